```python
import jax, jax.numpy as jnp
from jax import lax
import numpy as np

D_MODEL = 2048
BATCH = 4
SEQ = 4096
DEPTH = 1

CHUNK = 64
LEFT_CHUNKS = 8
BAND = (LEFT_CHUNKS + 1) * CHUNK
HEAD_DIM = 128
N_HEADS_A = 8
N_HEADS_B = 8
WIDTH_A = N_HEADS_A * HEAD_DIM
WIDTH_B = N_HEADS_B * HEAD_DIM
REL_CLIP = 256
REL_TABLE = REL_CLIP + CHUNK
Q_BLOCK = 128
N_EXPERTS = 32
TOP_K = 4
D_EXPERT = D_MODEL
SWIGLU_LIMIT = 7.0
SWIGLU_ALPHA = 1.702
MOE_BLOCK = 128
NORM_EPS = 1e-5
NEG_INF = -1e30
IN_COLS = 3 * WIDTH_A + 3 * WIDTH_B + N_HEADS_B + 2 * D_MODEL

kernel_name = 'hybrid_chunked_relpos_fox_moe_block'


def rmsnorm(x, w):
    xf = x.astype(jnp.float32)
    y = xf * lax.rsqrt(jnp.mean(xf * xf, axis=-1, keepdims=True) + NORM_EPS)
    return (y * w.astype(jnp.float32)).astype(x.dtype)


def chunked_relpos_attention(q, k, v, rel_bias):
    B, S, H, dh = q.shape
    nc = S // CHUNK
    pad = LEFT_CHUNKS * CHUNK
    k_pad = jnp.pad(k, ((0, 0), (pad, 0), (0, 0), (0, 0)))
    v_pad = jnp.pad(v, ((0, 0), (pad, 0), (0, 0), (0, 0)))
    qi = jnp.arange(CHUNK)[:, None]
    kj = jnp.arange(BAND)[None, :]
    rel = qi - kj + pad
    rel_idx = jnp.clip(rel, -(CHUNK - 1), REL_CLIP) + (CHUNK - 1)
    bias = rel_bias[:, rel_idx].astype(jnp.float32)
    scale = HEAD_DIM ** -0.5

    def one_chunk(c):
        start = c * CHUNK
        qc = lax.dynamic_slice_in_dim(q, start, CHUNK, axis=1)
        kb = lax.dynamic_slice_in_dim(k_pad, start, BAND, axis=1)
        vb = lax.dynamic_slice_in_dim(v_pad, start, BAND, axis=1)
        s = jnp.einsum('bqhd,bkhd->bhqk', qc, kb, preferred_element_type=jnp.float32) * scale + bias[None]
        valid = (start - pad + jnp.arange(BAND)) >= 0
        s = jnp.where(valid[None, None, None, :], s, NEG_INF)
        p = jax.nn.softmax(s, axis=-1)
        return jnp.einsum('bhqk,bkhd->bqhd', p.astype(vb.dtype), vb)

    out = lax.map(one_chunk, jnp.arange(nc))
    return out.transpose(1, 0, 2, 3, 4).reshape(B, S, H * dh)


def forgetting_attention(q, k, v, forget_logit):
    B, S, H, dh = q.shape
    nb = S // Q_BLOCK
    log_f = jax.nn.log_sigmoid(forget_logit.astype(jnp.float32))
    cum = jnp.cumsum(log_f, axis=1).transpose(0, 2, 1)
    kpos = jnp.arange(S)
    scale = HEAD_DIM ** -0.5

    def one_block(blk):
        start = blk * Q_BLOCK
        qb = lax.dynamic_slice_in_dim(q, start, Q_BLOCK, axis=1)
        cq = lax.dynamic_slice_in_dim(cum, start, Q_BLOCK, axis=2)
        s = jnp.einsum('bqhd,bkhd->bhqk', qb, k, preferred_element_type=jnp.float32) * scale
        s = s + cq[..., :, None] - cum[..., None, :]
        qpos = start + jnp.arange(Q_BLOCK)
        s = jnp.where((kpos[None, :] <= qpos[:, None])[None, None], s, NEG_INF)
        p = jax.nn.softmax(s, axis=-1)
        return jnp.einsum('bhqk,bkhd->bqhd', p.astype(v.dtype), v)

    out = lax.map(one_block, jnp.arange(nb))
    return out.transpose(1, 0, 2, 3, 4).reshape(B, S, H * dh)


def moe_ffn(xn, w_router, b_router, w_gate_up, b_gate_up, w_down, b_down):
    B, S, D = xn.shape
    N = B * S
    A = N * TOP_K
    xf = xn.reshape(N, D)
    logits = (xf @ w_router + b_router).astype(jnp.float32)
    top_val, top_idx = lax.top_k(logits, TOP_K)
    gates = jax.nn.softmax(top_val, axis=-1)
    exp_flat = top_idx.reshape(A).astype(jnp.int32)
    tok_flat = jnp.repeat(jnp.arange(N, dtype=jnp.int32), TOP_K)
    gate_flat = gates.reshape(A)
    order = jnp.argsort(exp_flat)
    exp_sorted = exp_flat[order]
    counts = jnp.bincount(exp_flat, length=N_EXPERTS)
    starts = jnp.cumsum(counts) - counts
    padded = (counts + MOE_BLOCK - 1) // MOE_BLOCK * MOE_BLOCK
    pends = jnp.cumsum(padded)
    pstarts = pends - padded
    dest = pstarts[exp_sorted] + (jnp.arange(A) - starts[exp_sorted])
    P = A + N_EXPERTS * MOE_BLOCK
    nblk = P // MOE_BLOCK
    buf_tok = jnp.full((P,), N, jnp.int32).at[dest].set(tok_flat[order])
    buf_gate = jnp.zeros((P,), jnp.float32).at[dest].set(gate_flat[order])
    blk_expert = jnp.minimum(jnp.searchsorted(pends, jnp.arange(nblk) * MOE_BLOCK, side='right'), N_EXPERTS - 1).astype(jnp.int32)
    x_pad = jnp.concatenate([xf, jnp.zeros((1, D), xf.dtype)], axis=0)

    def one_block(args):
        tok, e = args
        xb = x_pad[tok]
        gu = xb @ w_gate_up[e] + b_gate_up[e]
        g, u = jnp.split(gu, 2, axis=-1)
        g = jnp.minimum(g, SWIGLU_LIMIT)
        u = jnp.clip(u, -SWIGLU_LIMIT, SWIGLU_LIMIT)
        hb = (u + 1.0) * (g * jax.nn.sigmoid(SWIGLU_ALPHA * g))
        return hb @ w_down[e] + b_down[e]

    y = lax.map(one_block, (buf_tok.reshape(nblk, MOE_BLOCK), blk_expert))
    y = y.reshape(P, D) * buf_gate[:, None].astype(y.dtype)
    out = jax.ops.segment_sum(y, buf_tok, num_segments=N + 1)[:N]
    return out.reshape(B, S, D)


def setup_inputs(seed: int = 0) -> dict:
    key = jax.random.key(seed)
    ks = jax.random.split(key, 24)
    f32 = jnp.float32
    nrm = lambda k, shape, s: (jax.random.normal(k, shape, f32) * s)
    return {
        'x': nrm(ks[0], (BATCH, SEQ, D_MODEL), 1.0),
        'attn_norm_w': 1.0 + nrm(ks[1], (D_MODEL,), 0.1),
        'w_in': nrm(ks[2], (D_MODEL, IN_COLS), D_MODEL ** -0.5),
        'b_forget': jax.random.uniform(ks[3], (N_HEADS_B,), f32, 1.0, 6.0),
        'qn_a': 1.0 + nrm(ks[4], (HEAD_DIM,), 0.1),
        'kn_a': 1.0 + nrm(ks[5], (HEAD_DIM,), 0.1),
        'qn_b': 1.0 + nrm(ks[6], (HEAD_DIM,), 0.1),
        'kn_b': 1.0 + nrm(ks[7], (HEAD_DIM,), 0.1),
        'rel_bias': nrm(ks[8], (N_HEADS_A, REL_TABLE), 0.5),
        'w_branch_a': nrm(ks[9], (WIDTH_A, D_MODEL), WIDTH_A ** -0.5),
        'w_branch_b': nrm(ks[10], (WIDTH_B, D_MODEL), WIDTH_B ** -0.5),
        'w_out': nrm(ks[11], (D_MODEL, D_MODEL), D_MODEL ** -0.5),
        'ffn_norm_w': 1.0 + nrm(ks[12], (D_MODEL,), 0.1),
        'w_router': nrm(ks[13], (D_MODEL, N_EXPERTS), D_MODEL ** -0.5),
        'b_router': nrm(ks[14], (N_EXPERTS,), 0.01),
        'w_gate_up': nrm(ks[15], (N_EXPERTS, D_MODEL, 2 * D_EXPERT), D_MODEL ** -0.5),
        'b_gate_up': nrm(ks[16], (N_EXPERTS, 2 * D_EXPERT), 0.02),
        'w_down': nrm(ks[17], (N_EXPERTS, D_EXPERT, D_MODEL), D_EXPERT ** -0.5),
        'b_down': nrm(ks[18], (N_EXPERTS, D_MODEL), 0.02),
    }


def reference(x, attn_norm_w, w_in, b_forget, qn_a, kn_a, qn_b, kn_b, rel_bias,
              w_branch_a, w_branch_b, w_out, ffn_norm_w, w_router, b_router,
              w_gate_up, b_gate_up, w_down, b_down):
    B, S, D = x.shape
    h = x
    for _ in range(DEPTH):
        xn = rmsnorm(h, attn_norm_w)
        proj = jnp.einsum('bsd,dc->bsc', xn, w_in)
        cuts = np.cumsum([WIDTH_A, WIDTH_A, WIDTH_A, WIDTH_B, WIDTH_B, WIDTH_B, N_HEADS_B, D_MODEL])
        qa, ka, va, qb, kb, vb, fb, ga, gb = jnp.split(proj, [int(c) for c in cuts], axis=-1)
        heads = lambda t, H: t.reshape(B, S, H, HEAD_DIM)
        qa = rmsnorm(heads(qa, N_HEADS_A), qn_a)
        ka = rmsnorm(heads(ka, N_HEADS_A), kn_a)
        qb = rmsnorm(heads(qb, N_HEADS_B), qn_b)
        kb = rmsnorm(heads(kb, N_HEADS_B), kn_b)
        y_a = chunked_relpos_attention(qa, ka, heads(va, N_HEADS_A), rel_bias)
        y_b = forgetting_attention(qb, kb, heads(vb, N_HEADS_B), fb + b_forget)
        z = jax.nn.sigmoid(ga) * (y_a @ w_branch_a) + jax.nn.sigmoid(gb) * (y_b @ w_branch_b)
        h = h + z @ w_out
        h = h + moe_ffn(rmsnorm(h, ffn_norm_w), w_router, b_router, w_gate_up, b_gate_up, w_down, b_down)
    return h
```

```python
import functools

import jax
import jax.numpy as jnp
from jax import lax
from jax.experimental import pallas as pl
from jax.experimental.pallas import tpu as pltpu

D_MODEL = 2048
CHUNK = 64
LEFT_CHUNKS = 8
BAND = (LEFT_CHUNKS + 1) * CHUNK
HEAD_DIM = 128
N_HEADS = 8
WIDTH = N_HEADS * HEAD_DIM
REL_CLIP = 256
N_EXPERTS = 32
TOP_K = 4
D_EXPERT = D_MODEL
SWIGLU_LIMIT = 7.0
SWIGLU_ALPHA = 1.702
NORM_EPS = 1e-5
NEG_INF = -1e30

LANES = 128
VMEM_LIMIT = 56 * 1024 * 1024

TM_PROJ = 512
TN_PROJ = 1024
TQ_A = 256
WIN_A = TQ_A + LEFT_CHUNKS * CHUNK
TQ_B = 512
TK_B = 512
TM_MERGE = 256
TM_MOE = 512
TF_MOE = 256

F32 = jnp.float32
BF16 = jnp.bfloat16


def _dot(a, b):
    return jnp.dot(a, b, preferred_element_type=F32)


def _dot_nt(a, b):
    return lax.dot_general(a, b, (((1,), (1,)), ((), ())), preferred_element_type=F32)


def _split3(x):
    hi = x.astype(BF16)
    r = x - hi.astype(F32)
    mid = r.astype(BF16)
    lo = (r - mid.astype(F32)).astype(BF16)
    return hi, mid, lo


def _dot_f32x(x, w_hi, w_lo):
    hi, mid, _ = _split3(x)
    return _dot(hi, w_hi) + (_dot(hi, w_lo) + _dot(mid, w_hi))


def _in_proj_kernel(x_ref, nw_ref, w_ref, wf_hi_ref, wf_lo_ref, bf_ref, hn_ref, tri_ref,
                    out_ref, cum_ref, xn_sc, carry_sc, *, tiles_per_seq):
    i = pl.program_id(0)
    j = pl.program_id(1)

    @pl.when(j == 0)
    def _():
        x = x_ref[...]
        ms = jnp.mean(x * x, axis=-1, keepdims=True)
        xn = x * lax.rsqrt(ms + NORM_EPS) * nw_ref[...]
        xn_sc[...] = xn.astype(BF16)
        z = _dot_f32x(xn, wf_hi_ref[...], wf_lo_ref[...]) + bf_ref[...]
        logf = jnp.minimum(z, 0.0) - jnp.log1p(jnp.exp(-jnp.abs(z)))
        hi, mid, lo = _split3(logf)
        tri = tri_ref[...]
        c = _dot(tri, hi) + (_dot(tri, mid) + _dot(tri, lo))
        carry = jnp.where(i % tiles_per_seq == 0, 0.0, carry_sc[...])
        c = c + carry
        cum_ref[...] = c
        carry_sc[...] = c[-1:, :]

    acc = _dot(xn_sc[...], w_ref[...])
    is_norm = jnp.logical_and(j < 6, jnp.logical_and(j != 2, j != 5))

    @pl.when(is_norm)
    def _():
        w = hn_ref[pl.ds(j, 1), :]
        for h in range(N_HEADS):
            sl = slice(h * HEAD_DIM, (h + 1) * HEAD_DIM)
            t = acc[:, sl]
            ms = jnp.mean(t * t, axis=-1, keepdims=True)
            out_ref[:, sl] = (t * lax.rsqrt(ms + NORM_EPS) * w).astype(BF16)

    @pl.when(jnp.logical_or(j == 2, j == 5))
    def _():
        out_ref[...] = acc.astype(BF16)

    @pl.when(j >= 6)
    def _():
        out_ref[...] = jax.nn.sigmoid(acc).astype(BF16)


def _in_proj(x2, attn_norm_w, w_all, wf_hi, wf_lo, bf_pad, head_norm_w, seq):
    n, d = x2.shape
    n_col = w_all.shape[1] // TN_PROJ
    tm = TM_PROJ
    tri = (jnp.arange(tm)[:, None] >= jnp.arange(tm)[None, :]).astype(BF16)
    return pl.pallas_call(
        functools.partial(_in_proj_kernel, tiles_per_seq=seq // tm),
        name="in_proj",
        grid=(n // tm, n_col),
        in_specs=[
            pl.BlockSpec((tm, d), lambda i, j: (i, 0)),
            pl.BlockSpec((1, d), lambda i, j: (0, 0)),
            pl.BlockSpec((d, TN_PROJ), lambda i, j: (0, j)),
            pl.BlockSpec((d, LANES), lambda i, j: (0, 0)),
            pl.BlockSpec((d, LANES), lambda i, j: (0, 0)),
            pl.BlockSpec((1, LANES), lambda i, j: (0, 0)),
            pl.BlockSpec(head_norm_w.shape, lambda i, j: (0, 0)),
            pl.BlockSpec((tm, tm), lambda i, j: (0, 0)),
        ],
        out_specs=[
            pl.BlockSpec((tm, TN_PROJ), lambda i, j: (i, j)),
            pl.BlockSpec((tm, LANES), lambda i, j: (i, 0)),
        ],
        out_shape=[
            jax.ShapeDtypeStruct((n, w_all.shape[1]), BF16),
            jax.ShapeDtypeStruct((n, LANES), F32),
        ],
        scratch_shapes=[pltpu.VMEM((tm, d), BF16), pltpu.VMEM((1, LANES), F32)],
        compiler_params=pltpu.CompilerParams(
            dimension_semantics=("arbitrary", "arbitrary"), vmem_limit_bytes=VMEM_LIMIT),
    )(x2, attn_norm_w.reshape(1, d), w_all, wf_hi, wf_lo, bf_pad, head_norm_w, tri)


def _attn_a_kernel(q_ref, k0_ref, k1_ref, k2_ref, v0_ref, v1_ref, v2_ref, bias_ref, o_ref):
    qb = pl.program_id(1)
    k_refs = (k0_ref, k1_ref, k2_ref)
    v_refs = (v0_ref, v1_ref, v2_ref)
    n_win = len(k_refs)
    for h in range(N_HEADS):
        sl = slice(h * HEAD_DIM, (h + 1) * HEAD_DIM)
        q = q_ref[:, sl]
        scores = []
        for t in range(n_win):
            s = _dot_nt(q, k_refs[t][:, sl]) + bias_ref[h, :, t * TQ_A:(t + 1) * TQ_A]
            if t < n_win - 1:
                s = jnp.where(qb >= n_win - 1 - t, s, NEG_INF)
            scores.append(s)
        m = functools.reduce(jnp.maximum, [jnp.max(s, axis=-1, keepdims=True) for s in scores])
        ps = [jnp.exp(s - m) for s in scores]
        l = functools.reduce(jnp.add, [jnp.sum(p, axis=-1, keepdims=True) for p in ps])
        o = functools.reduce(jnp.add, [_dot(ps[t].astype(BF16), v_refs[t][:, sl]) for t in range(n_win)])
        o_ref[:, sl] = (o / l).astype(BF16)


def _attn_a_bias(rel_bias):
    r = jnp.arange(TQ_A)[:, None]
    c = jnp.arange(WIN_A)[None, :]
    lo = (r // CHUNK) * CHUNK
    in_band = jnp.logical_and(c >= lo, c < lo + BAND)
    rel = r - c + LEFT_CHUNKS * CHUNK
    rel_idx = jnp.clip(rel, -(CHUNK - 1), REL_CLIP) + (CHUNK - 1)
    bias = rel_bias[:, rel_idx].astype(F32)
    return jnp.where(in_band[None], bias, NEG_INF)


def _attn_a(proj3, bias):
    b, s, _ = proj3.shape
    nq = s // TQ_A
    n_win = WIN_A // TQ_A

    def kv_spec(group, t):
        return pl.BlockSpec((None, TQ_A, WIDTH),
                            lambda bi, qi: (bi, jnp.maximum(qi - (n_win - 1) + t, 0), group))

    return pl.pallas_call(
        _attn_a_kernel,
        name="attn_a",
        grid=(b, nq),
        in_specs=[pl.BlockSpec((None, TQ_A, WIDTH), lambda bi, qi: (bi, qi, 0))]
        + [kv_spec(1, t) for t in range(n_win)]
        + [kv_spec(2, t) for t in range(n_win)]
        + [pl.BlockSpec(bias.shape, lambda bi, qi: (0, 0, 0))],
        out_specs=pl.BlockSpec((None, TQ_A, WIDTH), lambda bi, qi: (bi, qi, 0)),
        out_shape=jax.ShapeDtypeStruct((b, s, WIDTH), BF16),
        compiler_params=pltpu.CompilerParams(
            dimension_semantics=("arbitrary", "arbitrary"), vmem_limit_bytes=VMEM_LIMIT),
    )(proj3, *([proj3] * (2 * n_win)), bias)


def _attn_b_kernel(q_ref, k_ref, v_ref, cq_ref, ck_ref, o_ref, m_sc, l_sc, acc_sc):
    h = pl.program_id(1)
    qi = pl.program_id(2)
    q = q_ref[...]
    lane = lax.broadcasted_iota(jnp.int32, (TQ_B, LANES), 1)
    cq = jnp.sum(jnp.where(lane == h, cq_ref[...], 0.0), axis=-1, keepdims=True)
    m_sc[...] = jnp.full((TQ_B, 1), NEG_INF, F32)
    l_sc[...] = jnp.zeros((TQ_B, 1), F32)
    acc_sc[...] = jnp.zeros((TQ_B, HEAD_DIM), F32)

    def block(kb, masked):
        k = k_ref[pl.ds(kb * TK_B, TK_B), :]
        v = v_ref[pl.ds(kb * TK_B, TK_B), :]
        s = _dot_nt(q, k) + (cq - ck_ref[kb])
        if masked:
            row = lax.broadcasted_iota(jnp.int32, (TQ_B, TK_B), 0)
            col = lax.broadcasted_iota(jnp.int32, (TQ_B, TK_B), 1)
            s = jnp.where(col <= row, s, NEG_INF)
        m_old = m_sc[...]
        m_new = jnp.maximum(m_old, jnp.max(s, axis=-1, keepdims=True))
        alpha = jnp.exp(m_old - m_new)
        p = jnp.exp(s - m_new)
        l_sc[...] = alpha * l_sc[...] + jnp.sum(p, axis=-1, keepdims=True)
        acc_sc[...] = alpha * acc_sc[...] + _dot(p.astype(BF16), v)
        m_sc[...] = m_new

    def body(kb, carry):
        block(kb, False)
        return carry

    lax.fori_loop(0, qi, body, 0)
    block(qi, True)
    o_ref[...] = (acc_sc[...] / l_sc[...]).astype(BF16)


def _attn_b(proj3, cum3, cum_rows):
    b, s, _ = proj3.shape
    nq = s // TQ_B
    col0 = 3 * N_HEADS
    return pl.pallas_call(
        _attn_b_kernel,
        name="attn_b",
        grid=(b, N_HEADS, nq),
        in_specs=[
            pl.BlockSpec((None, TQ_B, HEAD_DIM), lambda bi, h, qi: (bi, qi, col0 + h)),
            pl.BlockSpec((None, s, HEAD_DIM), lambda bi, h, qi: (bi, 0, col0 + N_HEADS + h)),
            pl.BlockSpec((None, s, HEAD_DIM), lambda bi, h, qi: (bi, 0, col0 + 2 * N_HEADS + h)),
            pl.BlockSpec((None, TQ_B, LANES), lambda bi, h, qi: (bi, qi, 0)),
            pl.BlockSpec((None, None, s // TK_B, 1, TK_B), lambda bi, h, qi: (bi, h, 0, 0, 0)),
        ],
        out_specs=pl.BlockSpec((None, TQ_B, HEAD_DIM), lambda bi, h, qi: (bi, qi, h)),
        out_shape=jax.ShapeDtypeStruct((b, s, WIDTH), BF16),
        scratch_shapes=[pltpu.VMEM((TQ_B, 1), F32), pltpu.VMEM((TQ_B, 1), F32),
                        pltpu.VMEM((TQ_B, HEAD_DIM), F32)],
        compiler_params=pltpu.CompilerParams(
            dimension_semantics=("arbitrary", "arbitrary", "arbitrary"), vmem_limit_bytes=VMEM_LIMIT),
    )(proj3, proj3, proj3, cum3, cum_rows)


def _merge_kernel(x_ref, ya_ref, yb_ref, ga_ref, gb_ref, wa_ref, wb_ref, wo_ref, fw_ref,
                  wr_hi_ref, wr_lo_ref, br_ref, tri_ref,
                  h_ref, hn_ref, route_ref, cnt_ref, carry_sc):
    i = pl.program_id(0)

    @pl.when(i == 0)
    def _():
        carry_sc[...] = jnp.zeros_like(carry_sc)

    za = _dot(ya_ref[...], wa_ref[...])
    zb = _dot(yb_ref[...], wb_ref[...])
    z = ga_ref[...].astype(F32) * za + gb_ref[...].astype(F32) * zb
    h = x_ref[...] + _dot(z.astype(BF16), wo_ref[...])
    h_ref[...] = h
    ms = jnp.mean(h * h, axis=-1, keepdims=True)
    hn = h * lax.rsqrt(ms + NORM_EPS) * fw_ref[...]
    hn_ref[...] = hn.astype(BF16)

    logits = _dot_f32x(hn, wr_hi_ref[...], wr_lo_ref[...]) + br_ref[...]
    tm = logits.shape[0]
    lane = lax.broadcasted_iota(jnp.int32, (tm, LANES), 1).astype(F32)
    work = logits
    vals, idxs = [], []
    for _ in range(TOP_K):
        m = jnp.max(work, axis=-1, keepdims=True)
        ix = jnp.min(jnp.where(work == m, lane, float(LANES)), axis=-1, keepdims=True)
        vals.append(m)
        idxs.append(ix)
        work = jnp.where(lane == ix, -jnp.inf, work)
    es = [jnp.exp(v - vals[0]) for v in vals]
    denom = functools.reduce(jnp.add, es)
    onehots = [(lane == ix).astype(F32) for ix in idxs]
    cnt = functools.reduce(jnp.add, onehots)
    before = _dot(tri_ref[...], cnt.astype(BF16)) + carry_sc[...]
    route = jnp.zeros((tm, LANES), F32)
    for k in range(TOP_K):
        rank = jnp.sum(onehots[k] * before, axis=-1, keepdims=True)
        route = jnp.where(lane == float(k), idxs[k], route)
        route = jnp.where(lane == float(TOP_K + k), es[k] / denom, route)
        route = jnp.where(lane == float(2 * TOP_K + k), rank, route)
    route_ref[...] = route
    total = carry_sc[...] + jnp.sum(cnt, axis=0, keepdims=True)
    carry_sc[...] = total
    cnt_ref[...] = total


def _merge(x2, ya, yb, proj, w_a, w_b, w_o, ffn_norm_w, wr_hi, wr_lo, br_pad):
    n, d = x2.shape
    tm = TM_MERGE
    tri = (jnp.arange(tm)[:, None] > jnp.arange(tm)[None, :]).astype(BF16)
    ga_blk = 3 * WIDTH * 2 // d
    const = lambda i: (0, 0)
    return pl.pallas_call(
        _merge_kernel,
        name="merge",
        grid=(n // tm,),
        in_specs=[
            pl.BlockSpec((tm, d), lambda i: (i, 0)),
            pl.BlockSpec((tm, WIDTH), lambda i: (i, 0)),
            pl.BlockSpec((tm, WIDTH), lambda i: (i, 0)),
            pl.BlockSpec((tm, d), lambda i: (i, ga_blk)),
            pl.BlockSpec((tm, d), lambda i: (i, ga_blk + 1)),
            pl.BlockSpec((WIDTH, d), const, pipeline_mode=pl.Buffered(1)),
            pl.BlockSpec((WIDTH, d), const, pipeline_mode=pl.Buffered(1)),
            pl.BlockSpec((d, d), const, pipeline_mode=pl.Buffered(1)),
            pl.BlockSpec((1, d), const),
            pl.BlockSpec((d, LANES), const),
            pl.BlockSpec((d, LANES), const),
            pl.BlockSpec((1, LANES), const),
            pl.BlockSpec((tm, tm), const),
        ],
        out_specs=[
            pl.BlockSpec((tm, d), lambda i: (i, 0)),
            pl.BlockSpec((tm, d), lambda i: (i, 0)),
            pl.BlockSpec((tm, LANES), lambda i: (i, 0)),
            pl.BlockSpec((1, LANES), const),
        ],
        out_shape=[
            jax.ShapeDtypeStruct((n, d), F32),
            jax.ShapeDtypeStruct((n, d), BF16),
            jax.ShapeDtypeStruct((n, LANES), F32),
            jax.ShapeDtypeStruct((1, LANES), F32),
        ],
        scratch_shapes=[pltpu.VMEM((1, LANES), F32)],
        compiler_params=pltpu.CompilerParams(
            dimension_semantics=("arbitrary",), vmem_limit_bytes=VMEM_LIMIT),
    )(x2, ya, yb, proj, proj, w_a, w_b, w_o, ffn_norm_w.reshape(1, d), wr_hi, wr_lo, br_pad, tri)


def _moe_kernel(be_ref, nb_ref, x_ref, wg_ref, wu_ref, wd_ref, bg_ref, bu_ref, bd_ref, o_ref, acc_sc):
    i = pl.program_id(0)
    j = pl.program_id(1)

    @pl.when(i < nb_ref[0])
    def _():
        x = x_ref[...]
        g = _dot(x, wg_ref[...].astype(BF16)) + bg_ref[...]
        u = _dot(x, wu_ref[...].astype(BF16)) + bu_ref[...]
        g = jnp.minimum(g, SWIGLU_LIMIT)
        u = jnp.clip(u, -SWIGLU_LIMIT, SWIGLU_LIMIT)
        hb = (u + 1.0) * (g * jax.nn.sigmoid(SWIGLU_ALPHA * g))
        part = _dot(hb.astype(BF16), wd_ref[...].astype(BF16))

        @pl.when(j == 0)
        def _():
            acc_sc[...] = part

        @pl.when(j > 0)
        def _():
            acc_sc[...] += part

        @pl.when(j == pl.num_programs(1) - 1)
        def _():
            o_ref[...] = (acc_sc[...] + bd_ref[...]).astype(BF16)


def _moe(xs, blk_expert, n_blocks, w_gate_up, b_gate_up, w_down, b_down):
    p, d = xs.shape
    nj = D_EXPERT // TF_MOE
    b_gu = b_gate_up.reshape(N_EXPERTS, 1, 2 * D_EXPERT)
    b_dn = b_down.reshape(N_EXPERTS, 1, d)
    grid_spec = pltpu.PrefetchScalarGridSpec(
        num_scalar_prefetch=2,
        grid=(p // TM_MOE, nj),
        in_specs=[
            pl.BlockSpec((TM_MOE, d), lambda i, j, be, nb: (i, 0)),
            pl.BlockSpec((None, d, TF_MOE), lambda i, j, be, nb: (be[i], 0, j)),
            pl.BlockSpec((None, d, TF_MOE), lambda i, j, be, nb: (be[i], 0, j + nj)),
            pl.BlockSpec((None, TF_MOE, d), lambda i, j, be, nb: (be[i], j, 0)),
            pl.BlockSpec((None, 1, TF_MOE), lambda i, j, be, nb: (be[i], 0, j)),
            pl.BlockSpec((None, 1, TF_MOE), lambda i, j, be, nb: (be[i], 0, j + nj)),
            pl.BlockSpec((None, 1, d), lambda i, j, be, nb: (be[i], 0, 0)),
        ],
        out_specs=pl.BlockSpec((TM_MOE, d), lambda i, j, be, nb: (i, 0)),
        scratch_shapes=[pltpu.VMEM((TM_MOE, d), F32)],
    )
    return pl.pallas_call(
        _moe_kernel,
        name="moe",
        grid_spec=grid_spec,
        out_shape=jax.ShapeDtypeStruct((p, d), BF16),
        compiler_params=pltpu.CompilerParams(
            dimension_semantics=("arbitrary", "arbitrary"), vmem_limit_bytes=VMEM_LIMIT),
    )(blk_expert, n_blocks, xs, w_gate_up, w_gate_up, w_down, b_gu, b_gu, b_dn)


def _pad_lanes(a):
    return jnp.pad(a, ((0, 0), (0, LANES - a.shape[1])))


def kernel(x, attn_norm_w, w_in, b_forget, qn_a, kn_a, qn_b, kn_b, rel_bias, w_branch_a, w_branch_b,
           w_out, ffn_norm_w, w_router, b_router, w_gate_up, b_gate_up, w_down, b_down):
    b, s, d = x.shape
    n = b * s
    x2 = x.reshape(n, d)

    n_qkv = 6 * WIDTH
    w_all = jnp.concatenate([w_in[:, :n_qkv], w_in[:, n_qkv + N_HEADS:]], axis=1).astype(BF16)
    w_f = _pad_lanes(w_in[:, n_qkv:n_qkv + N_HEADS])
    wf_hi = w_f.astype(BF16)
    wf_lo = (w_f - wf_hi.astype(F32)).astype(BF16)
    bf_pad = _pad_lanes(b_forget.reshape(1, N_HEADS))
    scale = HEAD_DIM ** -0.5
    ones = jnp.ones((HEAD_DIM,), F32)
    head_norm_w = jnp.stack([qn_a * scale, kn_a, ones, qn_b * scale, kn_b, ones, ones, ones])
    wr = _pad_lanes(w_router)
    wr_hi = wr.astype(BF16)
    wr_lo = (wr - wr_hi.astype(F32)).astype(BF16)
    br_pad = jnp.concatenate([b_router.reshape(1, N_EXPERTS),
                              jnp.full((1, LANES - N_EXPERTS), NEG_INF, F32)], axis=1)

    proj, cum = _in_proj(x2, attn_norm_w, w_all, wf_hi, wf_lo, bf_pad, head_norm_w, s)
    proj3 = proj.reshape(b, s, proj.shape[1])

    y_a = _attn_a(proj3, _attn_a_bias(rel_bias))
    cum3 = cum.reshape(b, s, LANES)
    cum_rows = jnp.transpose(cum3[:, :, :N_HEADS], (0, 2, 1)).reshape(b, N_HEADS, s // TK_B, 1, TK_B)
    y_b = _attn_b(proj3, cum3, cum_rows)

    h, hn, route, cnt = _merge(x2, y_a.reshape(n, WIDTH), y_b.reshape(n, WIDTH), proj,
                               w_branch_a.astype(BF16), w_branch_b.astype(BF16), w_out.astype(BF16),
                               ffn_norm_w, wr_hi, wr_lo, br_pad)

    top_e = route[:, :TOP_K].astype(jnp.int32)
    gates = route[:, TOP_K:2 * TOP_K]
    rank = route[:, 2 * TOP_K:3 * TOP_K].astype(jnp.int32)
    counts = cnt[0, :N_EXPERTS].astype(jnp.int32)
    padded = (counts + TM_MOE - 1) // TM_MOE * TM_MOE
    pends = jnp.cumsum(padded)
    pstarts = pends - padded
    dest = pstarts[top_e] + rank
    p_rows = n * TOP_K + N_EXPERTS * TM_MOE
    n_blk = p_rows // TM_MOE
    blk_expert = jnp.minimum(jnp.searchsorted(pends, jnp.arange(n_blk) * TM_MOE, side="right"),
                             N_EXPERTS - 1).astype(jnp.int32)
    n_blocks = (pends[-1] // TM_MOE).astype(jnp.int32).reshape(1)

    xs = jnp.zeros((p_rows, d), BF16).at[dest.reshape(-1)].set(jnp.repeat(hn, TOP_K, axis=0))
    ys = _moe(xs, blk_expert, n_blocks, w_gate_up, b_gate_up, w_down, b_down)
    y_tok = ys[dest.reshape(-1)].reshape(n, TOP_K, d).astype(F32)
    out = h + jnp.sum(y_tok * gates[:, :, None], axis=1)
    return out.reshape(b, s, d)
```

```python
import functools

import jax
import jax.numpy as jnp
from jax import lax
from jax.experimental import pallas as pl
from jax.experimental.pallas import tpu as pltpu

D_MODEL = 2048
CHUNK = 64
LEFT_CHUNKS = 8
BAND = (LEFT_CHUNKS + 1) * CHUNK
HEAD_DIM = 128
N_HEADS = 8
WIDTH = N_HEADS * HEAD_DIM
REL_CLIP = 256
N_EXPERTS = 32
TOP_K = 4
D_EXPERT = D_MODEL
SWIGLU_LIMIT = 7.0
SWIGLU_ALPHA = 1.702
NORM_EPS = 1e-5
NEG_INF = -1e30

LANES = 128
SUBLANES = 8
VMEM_LIMIT = 56 * 1024 * 1024

TM_PROJ = 512
TN_PROJ = 1024
TQ_A = 256
WIN_A = TQ_A + LEFT_CHUNKS * CHUNK
TQ_B = 512
TK_B = 512
HB_B = 2
TM_MERGE = 256
TM_ROUTE = 256
TM_MOE = 512
TF_MOE = 256

F32 = jnp.float32
BF16 = jnp.bfloat16


def _dot(a, b):
    return jnp.dot(a, b, preferred_element_type=F32)


def _dot_nt(a, b):
    return lax.dot_general(a, b, (((1,), (1,)), ((), ())), preferred_element_type=F32)


def _split3(x):
    hi = x.astype(BF16)
    r = x - hi.astype(F32)
    mid = r.astype(BF16)
    lo = (r - mid.astype(F32)).astype(BF16)
    return hi, mid, lo


def _dot_f32x(x, w_hi, w_lo):
    hi, mid, _ = _split3(x)
    return _dot(hi, w_hi) + (_dot(hi, w_lo) + _dot(mid, w_hi))


def _in_proj_kernel(x_ref, nw_ref, w_ref, wf_hi_ref, wf_lo_ref, bf_ref, hn_ref, tri_ref,
                    out_ref, cum_ref, xn_sc, carry_sc, *, tiles_per_seq):
    i = pl.program_id(0)
    j = pl.program_id(1)

    @pl.when(j == 0)
    def _():
        x = x_ref[...]
        ms = jnp.mean(x * x, axis=-1, keepdims=True)
        xn = x * lax.rsqrt(ms + NORM_EPS) * nw_ref[...]
        xn_sc[...] = xn.astype(BF16)
        z = _dot_f32x(xn, wf_hi_ref[...], wf_lo_ref[...]) + bf_ref[...]
        logf = jnp.minimum(z, 0.0) - jnp.log1p(jnp.exp(-jnp.abs(z)))
        hi, mid, lo = _split3(logf)
        tri = tri_ref[...]
        c = _dot(tri, hi) + (_dot(tri, mid) + _dot(tri, lo))
        carry = jnp.where(i % tiles_per_seq == 0, 0.0, carry_sc[...])
        c = c + carry
        cum_ref[...] = c
        carry_sc[...] = c[-1:, :]

    acc = _dot(xn_sc[...], w_ref[...])
    is_norm = jnp.logical_and(j < 6, jnp.logical_and(j != 2, j != 5))

    @pl.when(is_norm)
    def _():
        w = hn_ref[pl.ds(j, 1), :]
        for h in range(N_HEADS):
            sl = slice(h * HEAD_DIM, (h + 1) * HEAD_DIM)
            t = acc[:, sl]
            ms = jnp.mean(t * t, axis=-1, keepdims=True)
            out_ref[:, sl] = (t * lax.rsqrt(ms + NORM_EPS) * w).astype(BF16)

    @pl.when(jnp.logical_or(j == 2, j == 5))
    def _():
        out_ref[...] = acc.astype(BF16)

    @pl.when(j >= 6)
    def _():
        out_ref[...] = jax.nn.sigmoid(acc).astype(BF16)


def _in_proj(x2, attn_norm_w, w_all, wf_hi, wf_lo, bf_pad, head_norm_w, seq):
    n, d = x2.shape
    n_col = w_all.shape[1] // TN_PROJ
    tm = TM_PROJ
    tri = (jnp.arange(tm)[:, None] >= jnp.arange(tm)[None, :]).astype(BF16)
    return pl.pallas_call(
        functools.partial(_in_proj_kernel, tiles_per_seq=seq // tm),
        name="in_proj",
        grid=(n // tm, n_col),
        in_specs=[
            pl.BlockSpec((tm, d), lambda i, j: (i, 0)),
            pl.BlockSpec((1, d), lambda i, j: (0, 0)),
            pl.BlockSpec((d, TN_PROJ), lambda i, j: (0, j)),
            pl.BlockSpec((d, LANES), lambda i, j: (0, 0)),
            pl.BlockSpec((d, LANES), lambda i, j: (0, 0)),
            pl.BlockSpec((1, LANES), lambda i, j: (0, 0)),
            pl.BlockSpec(head_norm_w.shape, lambda i, j: (0, 0)),
            pl.BlockSpec((tm, tm), lambda i, j: (0, 0)),
        ],
        out_specs=[
            pl.BlockSpec((tm, TN_PROJ), lambda i, j: (i, j)),
            pl.BlockSpec((tm, LANES), lambda i, j: (i, 0)),
        ],
        out_shape=[
            jax.ShapeDtypeStruct((n, w_all.shape[1]), BF16),
            jax.ShapeDtypeStruct((n, LANES), F32),
        ],
        scratch_shapes=[pltpu.VMEM((tm, d), BF16), pltpu.VMEM((1, LANES), F32)],
        compiler_params=pltpu.CompilerParams(
            dimension_semantics=("arbitrary", "arbitrary"), vmem_limit_bytes=VMEM_LIMIT),
    )(x2, attn_norm_w.reshape(1, d), w_all, wf_hi, wf_lo, bf_pad, head_norm_w, tri)


def _attn_a_kernel(q_ref, k0_ref, k1_ref, k2_ref, v0_ref, v1_ref, v2_ref, bias_ref, o_ref):
    qb = pl.program_id(1)
    k_refs = (k0_ref, k1_ref, k2_ref)
    v_refs = (v0_ref, v1_ref, v2_ref)
    n_win = len(k_refs)
    for h in range(N_HEADS):
        sl = slice(h * HEAD_DIM, (h + 1) * HEAD_DIM)
        q = q_ref[:, sl]
        scores = []
        for t in range(n_win):
            s = _dot_nt(q, k_refs[t][:, sl]) + bias_ref[h, :, t * TQ_A:(t + 1) * TQ_A]
            if t < n_win - 1:
                s = jnp.where(qb >= n_win - 1 - t, s, NEG_INF)
            scores.append(s)
        m = functools.reduce(jnp.maximum, [jnp.max(s, axis=-1, keepdims=True) for s in scores])
        ps = [jnp.exp(s - m) for s in scores]
        l = functools.reduce(jnp.add, [jnp.sum(p, axis=-1, keepdims=True) for p in ps])
        o = functools.reduce(jnp.add, [_dot(ps[t].astype(BF16), v_refs[t][:, sl]) for t in range(n_win)])
        o_ref[:, sl] = (o / l).astype(BF16)


def _attn_a_bias(rel_bias):
    r = jnp.arange(TQ_A)[:, None]
    c = jnp.arange(WIN_A)[None, :]
    lo = (r // CHUNK) * CHUNK
    in_band = jnp.logical_and(c >= lo, c < lo + BAND)
    rel = r - c + LEFT_CHUNKS * CHUNK
    rel_idx = jnp.clip(rel, -(CHUNK - 1), REL_CLIP) + (CHUNK - 1)
    bias = rel_bias[:, rel_idx].astype(F32)
    return jnp.where(in_band[None], bias, NEG_INF)


def _attn_a(proj3, bias):
    b, s, _ = proj3.shape
    nq = s // TQ_A
    n_win = WIN_A // TQ_A

    def kv_spec(group, t):
        return pl.BlockSpec((None, TQ_A, WIDTH),
                            lambda bi, qi: (bi, jnp.maximum(qi - (n_win - 1) + t, 0), group))

    return pl.pallas_call(
        _attn_a_kernel,
        name="attn_a",
        grid=(b, nq),
        in_specs=[pl.BlockSpec((None, TQ_A, WIDTH), lambda bi, qi: (bi, qi, 0))]
        + [kv_spec(1, t) for t in range(n_win)]
        + [kv_spec(2, t) for t in range(n_win)]
        + [pl.BlockSpec(bias.shape, lambda bi, qi: (0, 0, 0))],
        out_specs=pl.BlockSpec((None, TQ_A, WIDTH), lambda bi, qi: (bi, qi, 0)),
        out_shape=jax.ShapeDtypeStruct((b, s, WIDTH), BF16),
        compiler_params=pltpu.CompilerParams(
            dimension_semantics=("arbitrary", "arbitrary"), vmem_limit_bytes=VMEM_LIMIT),
    )(proj3, *([proj3] * (2 * n_win)), bias)


def _attn_b_kernel(q_ref, k_ref, v_ref, cq_ref, ck_ref, o_ref, m_sc, l_sc, acc_sc):
    hp = pl.program_id(1)
    qi = pl.program_id(2)
    n_chunk = TK_B // LANES
    lane = lax.broadcasted_iota(jnp.int32, (TQ_B, LANES), 1)
    row = lax.broadcasted_iota(jnp.int32, (TQ_B, LANES), 0)
    qs, cqs = [], []
    for a in range(HB_B):
        qs.append(q_ref[:, a * HEAD_DIM:(a + 1) * HEAD_DIM])
        cq = jnp.sum(jnp.where(lane == hp * HB_B + a, cq_ref[...], 0.0), axis=-1, keepdims=True)
        cqs.append(jnp.broadcast_to(cq, (TQ_B, LANES)))
    m_sc[...] = jnp.full(m_sc.shape, NEG_INF, F32)
    l_sc[...] = jnp.zeros(l_sc.shape, F32)
    acc_sc[...] = jnp.zeros(acc_sc.shape, F32)

    def block(kb, masked):
        for a in range(HB_B):
            sl = slice(a * HEAD_DIM, (a + 1) * HEAD_DIM)
            k = k_ref[pl.ds(kb * TK_B, TK_B), sl]
            v = v_ref[pl.ds(kb * TK_B, TK_B), sl]
            s = _dot_nt(qs[a], k)
            ck = ck_ref[a, kb]
            chunks = []
            for c in range(n_chunk):
                cs = slice(c * LANES, (c + 1) * LANES)
                sc = s[:, cs] + (cqs[a] - ck[:, cs])
                if masked:
                    sc = jnp.where(lane + c * LANES <= row, sc, NEG_INF)
                chunks.append(sc)
            m_old = m_sc[a]
            m_new = jnp.maximum(m_old, jnp.max(functools.reduce(jnp.maximum, chunks), axis=-1, keepdims=True))
            alpha = jnp.exp(m_old - m_new)
            ps = [jnp.exp(sc - m_new) for sc in chunks]
            l_sc[a] = alpha * l_sc[a] + jnp.sum(functools.reduce(jnp.add, ps), axis=-1, keepdims=True)
            p = jnp.concatenate([pc.astype(BF16) for pc in ps], axis=1)
            acc_sc[a] = alpha * acc_sc[a] + _dot(p, v)
            m_sc[a] = m_new

    def body(kb, carry):
        block(kb, False)
        return carry

    lax.fori_loop(0, qi, body, 0)
    block(qi, True)
    for a in range(HB_B):
        o_ref[:, a * HEAD_DIM:(a + 1) * HEAD_DIM] = (acc_sc[a] / l_sc[a]).astype(BF16)


def _attn_b(proj3, cum3, cum_rows):
    b, s, _ = proj3.shape
    nq = s // TQ_B
    wb = HB_B * HEAD_DIM
    q0 = 3 * WIDTH // wb
    gstep = WIDTH // wb
    return pl.pallas_call(
        _attn_b_kernel,
        name="attn_b",
        grid=(b, N_HEADS // HB_B, nq),
        in_specs=[
            pl.BlockSpec((None, TQ_B, wb), lambda bi, hp, qi: (bi, qi, q0 + hp)),
            pl.BlockSpec((None, s, wb), lambda bi, hp, qi: (bi, 0, q0 + gstep + hp)),
            pl.BlockSpec((None, s, wb), lambda bi, hp, qi: (bi, 0, q0 + 2 * gstep + hp)),
            pl.BlockSpec((None, TQ_B, LANES), lambda bi, hp, qi: (bi, qi, 0)),
            pl.BlockSpec((None, HB_B, s // TK_B, 1, TK_B), lambda bi, hp, qi: (bi, hp, 0, 0, 0)),
        ],
        out_specs=pl.BlockSpec((None, TQ_B, wb), lambda bi, hp, qi: (bi, qi, hp)),
        out_shape=jax.ShapeDtypeStruct((b, s, WIDTH), BF16),
        scratch_shapes=[pltpu.VMEM((HB_B, TQ_B, LANES), F32), pltpu.VMEM((HB_B, TQ_B, LANES), F32),
                        pltpu.VMEM((HB_B, TQ_B, HEAD_DIM), F32)],
        compiler_params=pltpu.CompilerParams(
            dimension_semantics=("arbitrary", "arbitrary", "arbitrary"), vmem_limit_bytes=VMEM_LIMIT),
    )(proj3, proj3, proj3, cum3, cum_rows)


def _merge_kernel(x_ref, ya_ref, yb_ref, ga_ref, gb_ref, wa_ref, wb_ref, wo_ref, fw_ref,
                  wr_hi_ref, wr_lo_ref, br_ref, tri_ref,
                  h_ref, hn_ref, route_ref, cnt_ref, carry_sc):
    i = pl.program_id(0)

    @pl.when(i == 0)
    def _():
        carry_sc[...] = jnp.zeros_like(carry_sc)

    za = _dot(ya_ref[...], wa_ref[...])
    zb = _dot(yb_ref[...], wb_ref[...])
    z = ga_ref[...].astype(F32) * za + gb_ref[...].astype(F32) * zb
    h = x_ref[...] + _dot(z.astype(BF16), wo_ref[...])
    h_ref[...] = h
    ms = jnp.mean(h * h, axis=-1, keepdims=True)
    hn = h * lax.rsqrt(ms + NORM_EPS) * fw_ref[...]
    hn_ref[...] = hn

    logits = _dot_f32x(hn, wr_hi_ref[...], wr_lo_ref[...]) + br_ref[...]
    tm = logits.shape[0]
    lane = lax.broadcasted_iota(jnp.int32, (tm, LANES), 1).astype(F32)
    work = logits
    vals, idxs = [], []
    for _ in range(TOP_K):
        m = jnp.max(work, axis=-1, keepdims=True)
        ix = jnp.min(jnp.where(work == m, lane, float(LANES)), axis=-1, keepdims=True)
        vals.append(m)
        idxs.append(ix)
        work = jnp.where(lane == ix, -jnp.inf, work)
    es = [jnp.exp(v - vals[0]) for v in vals]
    denom = functools.reduce(jnp.add, es)
    onehots = [(lane == ix).astype(F32) for ix in idxs]
    cnt = functools.reduce(jnp.add, onehots)
    before = _dot(tri_ref[...], cnt.astype(BF16)) + carry_sc[...]
    route = jnp.zeros((tm, LANES), F32)
    for k in range(TOP_K):
        rank = jnp.sum(onehots[k] * before, axis=-1, keepdims=True)
        route = jnp.where(lane == float(k), idxs[k], route)
        route = jnp.where(lane == float(TOP_K + k), es[k] / denom, route)
        route = jnp.where(lane == float(2 * TOP_K + k), rank, route)
    route_ref[...] = route
    total = carry_sc[...] + jnp.sum(cnt, axis=0, keepdims=True)
    carry_sc[...] = total
    cnt_ref[...] = total


def _merge(x2, ya, yb, proj, w_a, w_b, w_o, ffn_norm_w, wr_hi, wr_lo, br_pad):
    n, d = x2.shape
    tm = TM_MERGE
    tri = (jnp.arange(tm)[:, None] > jnp.arange(tm)[None, :]).astype(BF16)
    ga_blk = 3 * WIDTH * 2 // d
    const = lambda i: (0, 0)
    return pl.pallas_call(
        _merge_kernel,
        name="merge",
        grid=(n // tm,),
        in_specs=[
            pl.BlockSpec((tm, d), lambda i: (i, 0)),
            pl.BlockSpec((tm, WIDTH), lambda i: (i, 0)),
            pl.BlockSpec((tm, WIDTH), lambda i: (i, 0)),
            pl.BlockSpec((tm, d), lambda i: (i, ga_blk)),
            pl.BlockSpec((tm, d), lambda i: (i, ga_blk + 1)),
            pl.BlockSpec((WIDTH, d), const, pipeline_mode=pl.Buffered(1)),
            pl.BlockSpec((WIDTH, d), const, pipeline_mode=pl.Buffered(1)),
            pl.BlockSpec((d, d), const, pipeline_mode=pl.Buffered(1)),
            pl.BlockSpec((1, d), const),
            pl.BlockSpec((d, LANES), const),
            pl.BlockSpec((d, LANES), const),
            pl.BlockSpec((1, LANES), const),
            pl.BlockSpec((tm, tm), const),
        ],
        out_specs=[
            pl.BlockSpec((tm, d), lambda i: (i, 0)),
            pl.BlockSpec((tm, d), lambda i: (i, 0)),
            pl.BlockSpec((tm, LANES), lambda i: (i, 0)),
            pl.BlockSpec((1, LANES), const),
        ],
        out_shape=[
            jax.ShapeDtypeStruct((n, d), F32),
            jax.ShapeDtypeStruct((n, d), F32),
            jax.ShapeDtypeStruct((n, LANES), F32),
            jax.ShapeDtypeStruct((1, LANES), F32),
        ],
        scratch_shapes=[pltpu.VMEM((1, LANES), F32)],
        compiler_params=pltpu.CompilerParams(
            dimension_semantics=("arbitrary",), vmem_limit_bytes=VMEM_LIMIT),
    )(x2, ya, yb, proj, proj, w_a, w_b, w_o, ffn_norm_w.reshape(1, d), wr_hi, wr_lo, br_pad, tri)


def _dispatch_kernel(cnt_ref, dest_ref, hn_ref, xs_ref, zero_sc, sem, zsem, *, cap):
    i = pl.program_id(0)

    def row_copy(t, k):
        return pltpu.make_async_copy(hn_ref.at[pl.ds(t, 1)], xs_ref.at[pl.ds(dest_ref[t * TOP_K + k], 1)], sem)

    def issue(t, carry):
        for k in range(TOP_K):
            row_copy(t, k).start()
        return carry

    lax.fori_loop(0, TM_ROUTE, issue, 0, unroll=8)

    @pl.when(i == pl.num_programs(0) - 1)
    def _():
        zero_sc[...] = jnp.zeros(zero_sc.shape, F32)

        def zero_fill(e, start):
            first = e * cap + cnt_ref[e]
            aligned = pl.multiple_of((first + SUBLANES - 1) // SUBLANES * SUBLANES, SUBLANES)
            for r in range(SUBLANES - 1):
                row = pltpu.make_async_copy(zero_sc.at[pl.ds(0, 1)], xs_ref.at[pl.ds(first + r, 1)], zsem)
                pl.when(first + r < aligned)(row.start if start else row.wait)
            block = pltpu.make_async_copy(zero_sc, xs_ref.at[pl.ds(aligned, TM_MOE)], zsem)
            block.start() if start else block.wait()

        for e in range(N_EXPERTS):
            zero_fill(e, True)
        for e in range(N_EXPERTS):
            zero_fill(e, False)

    def drain(t, carry):
        for k in range(TOP_K):
            row_copy(t, k).wait()
        return carry

    lax.fori_loop(0, TM_ROUTE, drain, 0, unroll=8)


def _dispatch(counts, dest_flat, hn, cap):
    n, d = hn.shape
    grid_spec = pltpu.PrefetchScalarGridSpec(
        num_scalar_prefetch=1,
        grid=(n // TM_ROUTE,),
        in_specs=[
            pl.BlockSpec((TM_ROUTE * TOP_K,), lambda i, cnt: (i,), memory_space=pltpu.SMEM),
            pl.BlockSpec((TM_ROUTE, d), lambda i, cnt: (i, 0)),
        ],
        out_specs=pl.BlockSpec(memory_space=pl.ANY),
        scratch_shapes=[pltpu.VMEM((TM_MOE, d), F32), pltpu.SemaphoreType.DMA, pltpu.SemaphoreType.DMA],
    )
    return pl.pallas_call(
        functools.partial(_dispatch_kernel, cap=cap),
        name="dispatch",
        grid_spec=grid_spec,
        out_shape=jax.ShapeDtypeStruct((N_EXPERTS * cap, d), F32),
        compiler_params=pltpu.CompilerParams(
            dimension_semantics=("arbitrary",), vmem_limit_bytes=VMEM_LIMIT),
    )(counts, dest_flat, hn)


def _moe_kernel(br_ref, be_ref, nb_ref, x_ref, wg_ref, wu_ref, wd_ref, bg_ref, bu_ref, bd_ref, o_ref,
                x_sc, acc_sc):
    i = pl.program_id(0)
    j = pl.program_id(1)

    @pl.when(i < nb_ref[0])
    def _():
        @pl.when(j == 0)
        def _():
            x_sc[...] = x_ref[...].astype(BF16)
            acc_sc[...] = jnp.broadcast_to(bd_ref[...], acc_sc.shape)

        x = x_sc[...]
        g = _dot(x, wg_ref[...].astype(BF16)) + bg_ref[...]
        u = _dot(x, wu_ref[...].astype(BF16)) + bu_ref[...]
        g = jnp.minimum(g, SWIGLU_LIMIT)
        u = jnp.clip(u, -SWIGLU_LIMIT, SWIGLU_LIMIT)
        hb = (u + 1.0) * (g * jax.nn.sigmoid(SWIGLU_ALPHA * g))
        acc_sc[...] += _dot(hb.astype(BF16), wd_ref[...].astype(BF16))

        @pl.when(j == pl.num_programs(1) - 1)
        def _():
            o_ref[...] = acc_sc[...]


def _moe(xs, blk_row, blk_expert, n_blocks, w_gate_up, b_gate_up, w_down, b_down, max_blocks):
    rows, d = xs.shape
    nj = D_EXPERT // TF_MOE
    b_gu = b_gate_up.reshape(N_EXPERTS, 1, 2 * D_EXPERT)
    b_dn = b_down.reshape(N_EXPERTS, 1, d)

    def jj(i, j, nb):
        return jnp.where(i < nb[0], j, nj - 1)

    grid_spec = pltpu.PrefetchScalarGridSpec(
        num_scalar_prefetch=3,
        grid=(max_blocks, nj),
        in_specs=[
            pl.BlockSpec((TM_MOE, d), lambda i, j, br, be, nb: (br[i], 0)),
            pl.BlockSpec((None, d, TF_MOE), lambda i, j, br, be, nb: (be[i], 0, jj(i, j, nb))),
            pl.BlockSpec((None, d, TF_MOE), lambda i, j, br, be, nb: (be[i], 0, jj(i, j, nb) + nj)),
            pl.BlockSpec((None, TF_MOE, d), lambda i, j, br, be, nb: (be[i], jj(i, j, nb), 0)),
            pl.BlockSpec((None, 1, TF_MOE), lambda i, j, br, be, nb: (be[i], 0, jj(i, j, nb))),
            pl.BlockSpec((None, 1, TF_MOE), lambda i, j, br, be, nb: (be[i], 0, jj(i, j, nb) + nj)),
            pl.BlockSpec((None, 1, d), lambda i, j, br, be, nb: (be[i], 0, 0)),
        ],
        out_specs=pl.BlockSpec((TM_MOE, d), lambda i, j, br, be, nb: (br[i], 0)),
        scratch_shapes=[pltpu.VMEM((TM_MOE, d), BF16), pltpu.VMEM((TM_MOE, d), F32)],
    )
    return pl.pallas_call(
        _moe_kernel,
        name="moe",
        grid_spec=grid_spec,
        out_shape=jax.ShapeDtypeStruct((rows, d), F32),
        compiler_params=pltpu.CompilerParams(
            dimension_semantics=("arbitrary", "arbitrary"), vmem_limit_bytes=VMEM_LIMIT),
    )(blk_row, blk_expert, n_blocks, xs, w_gate_up, w_gate_up, w_down, b_gu, b_gu, b_dn)


def _combine_kernel(dest_ref, h_ref, route_ref, ys_ref, o_ref, buf, sem):
    def row_copy(t, k):
        return pltpu.make_async_copy(ys_ref.at[pl.ds(dest_ref[t * TOP_K + k], 1)], buf.at[k, pl.ds(t, 1)], sem)

    def issue(t, carry):
        for k in range(TOP_K):
            row_copy(t, k).start()
        return carry

    def drain(t, carry):
        for k in range(TOP_K):
            row_copy(t, k).wait()
        return carry

    lax.fori_loop(0, TM_ROUTE, issue, 0, unroll=8)
    lax.fori_loop(0, TM_ROUTE, drain, 0, unroll=8)
    route = route_ref[...]
    acc = h_ref[...]
    for k in range(TOP_K):
        gate = route[:, TOP_K + k:TOP_K + k + 1]
        acc = acc + gate * buf[k]
    o_ref[...] = acc


def _combine(dest_flat, h, route, ys):
    n, d = h.shape
    return pl.pallas_call(
        _combine_kernel,
        name="combine",
        grid=(n // TM_ROUTE,),
        in_specs=[
            pl.BlockSpec((TM_ROUTE * TOP_K,), lambda i: (i,), memory_space=pltpu.SMEM),
            pl.BlockSpec((TM_ROUTE, d), lambda i: (i, 0)),
            pl.BlockSpec((TM_ROUTE, LANES), lambda i: (i, 0)),
            pl.BlockSpec(memory_space=pl.ANY),
        ],
        out_specs=pl.BlockSpec((TM_ROUTE, d), lambda i: (i, 0)),
        out_shape=jax.ShapeDtypeStruct((n, d), F32),
        scratch_shapes=[pltpu.VMEM((TOP_K, TM_ROUTE, d), F32), pltpu.SemaphoreType.DMA],
        compiler_params=pltpu.CompilerParams(
            dimension_semantics=("arbitrary",), vmem_limit_bytes=VMEM_LIMIT),
    )(dest_flat, h, route, ys)


def _pad_lanes(a):
    return jnp.pad(a, ((0, 0), (0, LANES - a.shape[1])))


def kernel(x, attn_norm_w, w_in, b_forget, qn_a, kn_a, qn_b, kn_b, rel_bias, w_branch_a, w_branch_b,
           w_out, ffn_norm_w, w_router, b_router, w_gate_up, b_gate_up, w_down, b_down):
    b, s, d = x.shape
    n = b * s
    x2 = x.reshape(n, d)

    n_qkv = 6 * WIDTH
    w_all = jnp.concatenate([w_in[:, :n_qkv], w_in[:, n_qkv + N_HEADS:]], axis=1).astype(BF16)
    w_f = _pad_lanes(w_in[:, n_qkv:n_qkv + N_HEADS])
    wf_hi = w_f.astype(BF16)
    wf_lo = (w_f - wf_hi.astype(F32)).astype(BF16)
    bf_pad = _pad_lanes(b_forget.reshape(1, N_HEADS))
    scale = HEAD_DIM ** -0.5
    ones = jnp.ones((HEAD_DIM,), F32)
    head_norm_w = jnp.stack([qn_a * scale, kn_a, ones, qn_b * scale, kn_b, ones, ones, ones])
    wr = _pad_lanes(w_router)
    wr_hi = wr.astype(BF16)
    wr_lo = (wr - wr_hi.astype(F32)).astype(BF16)
    br_pad = jnp.concatenate([b_router.reshape(1, N_EXPERTS),
                              jnp.full((1, LANES - N_EXPERTS), NEG_INF, F32)], axis=1)

    proj, cum = _in_proj(x2, attn_norm_w, w_all, wf_hi, wf_lo, bf_pad, head_norm_w, s)
    proj3 = proj.reshape(b, s, proj.shape[1])

    y_a = _attn_a(proj3, _attn_a_bias(rel_bias))
    cum3 = cum.reshape(b, s, LANES)
    cum_rows = jnp.transpose(cum3[:, :, :N_HEADS], (0, 2, 1)).reshape(b, N_HEADS, s // TK_B, 1, TK_B)
    y_b = _attn_b(proj3, cum3, cum_rows)

    h, hn, route, cnt = _merge(x2, y_a.reshape(n, WIDTH), y_b.reshape(n, WIDTH), proj,
                                w_branch_a.astype(BF16), w_branch_b.astype(BF16), w_out.astype(BF16),
                                ffn_norm_w, wr_hi, wr_lo, br_pad)

    cap = n + TM_MOE
    blocks_per_expert = cap // TM_MOE
    top_e = route[:, :TOP_K].astype(jnp.int32)
    rank = route[:, 2 * TOP_K:3 * TOP_K].astype(jnp.int32)
    dest_flat = (top_e * cap + rank).reshape(n * TOP_K)
    counts = cnt[0, :N_EXPERTS].astype(jnp.int32)
    nblk = (counts + TM_MOE - 1) // TM_MOE
    bend = jnp.cumsum(nblk)
    bstart = bend - nblk
    max_blocks = n * TOP_K // TM_MOE + N_EXPERTS
    blk = jnp.minimum(jnp.arange(max_blocks), bend[-1] - 1)
    blk_expert = jnp.minimum(jnp.searchsorted(bend, blk, side="right"), N_EXPERTS - 1).astype(jnp.int32)
    blk_row = (blk_expert * blocks_per_expert + (blk - bstart[blk_expert])).astype(jnp.int32)
    n_blocks = bend[-1].astype(jnp.int32).reshape(1)

    xs = _dispatch(counts, dest_flat, hn, cap)
    ys = _moe(xs, blk_row, blk_expert, n_blocks, w_gate_up, b_gate_up, w_down, b_down, max_blocks)
    out = _combine(dest_flat, h, route, ys)
    return out.reshape(b, s, d)
```

```python
import functools

import jax
import jax.numpy as jnp
from jax import lax
from jax.experimental import pallas as pl
from jax.experimental.pallas import tpu as pltpu

D_MODEL = 2048
CHUNK = 64
LEFT_CHUNKS = 8
BAND = (LEFT_CHUNKS + 1) * CHUNK
HEAD_DIM = 128
N_HEADS = 8
WIDTH = N_HEADS * HEAD_DIM
REL_CLIP = 256
N_EXPERTS = 32
TOP_K = 4
D_EXPERT = D_MODEL
SWIGLU_LIMIT = 7.0
SWIGLU_ALPHA = 1.702
NORM_EPS = 1e-5
NEG_INF = -1e30

LANES = 128
SUBLANES = 8
VMEM_LIMIT = 56 * 1024 * 1024

TM_PROJ = 512
TN_PROJ = 1024
TQ_A = 256
WIN_A = TQ_A + LEFT_CHUNKS * CHUNK
TQ_B = 512
TK_B = 512
HB_B = 2
TM_MERGE = 256
TM_ROUTE = 256
TM_MOE = 1024
TC_MOE = 256
TF_MOE = 256

F32 = jnp.float32
BF16 = jnp.bfloat16


def _dot(a, b):
    return jnp.dot(a, b, preferred_element_type=F32)


def _dot_nt(a, b):
    return lax.dot_general(a, b, (((1,), (1,)), ((), ())), preferred_element_type=F32)


def _split3(x):
    hi = x.astype(BF16)
    r = x - hi.astype(F32)
    mid = r.astype(BF16)
    lo = (r - mid.astype(F32)).astype(BF16)
    return hi, mid, lo


def _dot_f32x(x, w_hi, w_lo):
    hi, mid, _ = _split3(x)
    return _dot(hi, w_hi) + (_dot(hi, w_lo) + _dot(mid, w_hi))


def _in_proj_kernel(x_ref, nw_ref, w_ref, wf_hi_ref, wf_lo_ref, bf_ref, hn_ref, tri_ref,
                    out_ref, cum_ref, xn_sc, carry_sc, *, tiles_per_seq):
    i = pl.program_id(0)
    j = pl.program_id(1)

    @pl.when(j == 0)
    def _():
        x = x_ref[...]
        ms = jnp.mean(x * x, axis=-1, keepdims=True)
        xn = x * lax.rsqrt(ms + NORM_EPS) * nw_ref[...]
        xn_sc[...] = xn.astype(BF16)
        z = _dot_f32x(xn, wf_hi_ref[...], wf_lo_ref[...]) + bf_ref[...]
        logf = jnp.minimum(z, 0.0) - jnp.log1p(jnp.exp(-jnp.abs(z)))
        hi, mid, lo = _split3(logf)
        tri = tri_ref[...]
        c = _dot(tri, hi) + (_dot(tri, mid) + _dot(tri, lo))
        carry = jnp.where(i % tiles_per_seq == 0, 0.0, carry_sc[...])
        c = c + carry
        cum_ref[...] = c
        carry_sc[...] = c[-1:, :]

    acc = _dot(xn_sc[...], w_ref[...])
    is_norm = jnp.logical_and(j < 6, jnp.logical_and(j != 2, j != 5))

    @pl.when(is_norm)
    def _():
        w = hn_ref[pl.ds(j, 1), :]
        for h in range(N_HEADS):
            sl = slice(h * HEAD_DIM, (h + 1) * HEAD_DIM)
            t = acc[:, sl]
            ms = jnp.mean(t * t, axis=-1, keepdims=True)
            out_ref[:, sl] = (t * lax.rsqrt(ms + NORM_EPS) * w).astype(BF16)

    @pl.when(jnp.logical_or(j == 2, j == 5))
    def _():
        out_ref[...] = acc.astype(BF16)

    @pl.when(j >= 6)
    def _():
        out_ref[...] = jax.nn.sigmoid(acc).astype(BF16)


def _in_proj(x2, attn_norm_w, w_all, wf_hi, wf_lo, bf_pad, head_norm_w, seq):
    n, d = x2.shape
    n_col = w_all.shape[1] // TN_PROJ
    tm = TM_PROJ
    tri = (jnp.arange(tm)[:, None] >= jnp.arange(tm)[None, :]).astype(BF16)
    return pl.pallas_call(
        functools.partial(_in_proj_kernel, tiles_per_seq=seq // tm),
        name="in_proj",
        grid=(n // tm, n_col),
        in_specs=[
            pl.BlockSpec((tm, d), lambda i, j: (i, 0)),
            pl.BlockSpec((1, d), lambda i, j: (0, 0)),
            pl.BlockSpec((d, TN_PROJ), lambda i, j: (0, j)),
            pl.BlockSpec((d, LANES), lambda i, j: (0, 0)),
            pl.BlockSpec((d, LANES), lambda i, j: (0, 0)),
            pl.BlockSpec((1, LANES), lambda i, j: (0, 0)),
            pl.BlockSpec(head_norm_w.shape, lambda i, j: (0, 0)),
            pl.BlockSpec((tm, tm), lambda i, j: (0, 0)),
        ],
        out_specs=[
            pl.BlockSpec((tm, TN_PROJ), lambda i, j: (i, j)),
            pl.BlockSpec((tm, LANES), lambda i, j: (i, 0)),
        ],
        out_shape=[
            jax.ShapeDtypeStruct((n, w_all.shape[1]), BF16),
            jax.ShapeDtypeStruct((n, LANES), F32),
        ],
        scratch_shapes=[pltpu.VMEM((tm, d), BF16), pltpu.VMEM((1, LANES), F32)],
        compiler_params=pltpu.CompilerParams(
            dimension_semantics=("arbitrary", "arbitrary"), vmem_limit_bytes=VMEM_LIMIT),
    )(x2, attn_norm_w.reshape(1, d), w_all, wf_hi, wf_lo, bf_pad, head_norm_w, tri)


def _attn_a_kernel(q_ref, k0_ref, k1_ref, k2_ref, v0_ref, v1_ref, v2_ref, bias_ref, o_ref):
    qb = pl.program_id(1)
    k_refs = (k0_ref, k1_ref, k2_ref)
    v_refs = (v0_ref, v1_ref, v2_ref)
    n_win = len(k_refs)
    for h in range(N_HEADS):
        sl = slice(h * HEAD_DIM, (h + 1) * HEAD_DIM)
        q = q_ref[:, sl]
        scores = []
        for t in range(n_win):
            s = _dot_nt(q, k_refs[t][:, sl]) + bias_ref[h, :, t * TQ_A:(t + 1) * TQ_A]
            if t < n_win - 1:
                s = jnp.where(qb >= n_win - 1 - t, s, NEG_INF)
            scores.append(s)
        m = functools.reduce(jnp.maximum, [jnp.max(s, axis=-1, keepdims=True) for s in scores])
        ps = [jnp.exp(s - m) for s in scores]
        l = functools.reduce(jnp.add, [jnp.sum(p, axis=-1, keepdims=True) for p in ps])
        o = functools.reduce(jnp.add, [_dot(ps[t].astype(BF16), v_refs[t][:, sl]) for t in range(n_win)])
        o_ref[:, sl] = (o / l).astype(BF16)


def _attn_a_bias(rel_bias):
    r = jnp.arange(TQ_A)[:, None]
    c = jnp.arange(WIN_A)[None, :]
    lo = (r // CHUNK) * CHUNK
    in_band = jnp.logical_and(c >= lo, c < lo + BAND)
    rel = r - c + LEFT_CHUNKS * CHUNK
    rel_idx = jnp.clip(rel, -(CHUNK - 1), REL_CLIP) + (CHUNK - 1)
    onehot = (rel_idx[:, :, None] == jnp.arange(rel_bias.shape[1])[None, None, :]).astype(F32)
    bias = jnp.einsum("rct,ht->hrc", onehot, rel_bias.astype(F32), precision=lax.Precision.HIGHEST)
    return jnp.where(in_band[None], bias, NEG_INF)


def _attn_a(proj3, bias):
    b, s, _ = proj3.shape
    nq = s // TQ_A
    n_win = WIN_A // TQ_A

    def kv_spec(group, t):
        return pl.BlockSpec((None, TQ_A, WIDTH),
                            lambda bi, qi: (bi, jnp.maximum(qi - (n_win - 1) + t, 0), group))

    return pl.pallas_call(
        _attn_a_kernel,
        name="attn_a",
        grid=(b, nq),
        in_specs=[pl.BlockSpec((None, TQ_A, WIDTH), lambda bi, qi: (bi, qi, 0))]
        + [kv_spec(1, t) for t in range(n_win)]
        + [kv_spec(2, t) for t in range(n_win)]
        + [pl.BlockSpec(bias.shape, lambda bi, qi: (0, 0, 0))],
        out_specs=pl.BlockSpec((None, TQ_A, WIDTH), lambda bi, qi: (bi, qi, 0)),
        out_shape=jax.ShapeDtypeStruct((b, s, WIDTH), BF16),
        compiler_params=pltpu.CompilerParams(
            dimension_semantics=("arbitrary", "arbitrary"), vmem_limit_bytes=VMEM_LIMIT),
    )(proj3, *([proj3] * (2 * n_win)), bias)


def _attn_b_kernel(q_ref, k_ref, v_ref, cq_ref, ck_ref, o_ref, m_sc, l_sc, acc_sc):
    hp = pl.program_id(1)
    qi = pl.program_id(2)
    n_chunk = TK_B // LANES
    lane = lax.broadcasted_iota(jnp.int32, (TQ_B, LANES), 1)
    row = lax.broadcasted_iota(jnp.int32, (TQ_B, LANES), 0)
    qs, cqs = [], []
    for a in range(HB_B):
        qs.append(q_ref[:, a * HEAD_DIM:(a + 1) * HEAD_DIM])
        cq = jnp.sum(jnp.where(lane == hp * HB_B + a, cq_ref[...], 0.0), axis=-1, keepdims=True)
        cqs.append(jnp.broadcast_to(cq, (TQ_B, LANES)))
    m_sc[...] = jnp.full(m_sc.shape, NEG_INF, F32)
    l_sc[...] = jnp.zeros(l_sc.shape, F32)
    acc_sc[...] = jnp.zeros(acc_sc.shape, F32)

    def block(kb, masked):
        for a in range(HB_B):
            sl = slice(a * HEAD_DIM, (a + 1) * HEAD_DIM)
            k = k_ref[pl.ds(kb * TK_B, TK_B), sl]
            v = v_ref[pl.ds(kb * TK_B, TK_B), sl]
            s = _dot_nt(qs[a], k)
            ck = ck_ref[a, kb]
            chunks = []
            for c in range(n_chunk):
                cs = slice(c * LANES, (c + 1) * LANES)
                sc = s[:, cs] + (cqs[a] - ck[:, cs])
                if masked:
                    sc = jnp.where(lane + c * LANES <= row, sc, NEG_INF)
                chunks.append(sc)
            m_old = m_sc[a]
            m_new = jnp.maximum(m_old, jnp.max(functools.reduce(jnp.maximum, chunks), axis=-1, keepdims=True))
            alpha = jnp.exp(m_old - m_new)
            ps = [jnp.exp(sc - m_new) for sc in chunks]
            l_sc[a] = alpha * l_sc[a] + jnp.sum(functools.reduce(jnp.add, ps), axis=-1, keepdims=True)
            p = jnp.concatenate([pc.astype(BF16) for pc in ps], axis=1)
            acc_sc[a] = alpha * acc_sc[a] + _dot(p, v)
            m_sc[a] = m_new

    def body(kb, carry):
        block(kb, False)
        return carry

    lax.fori_loop(0, qi, body, 0)
    block(qi, True)
    for a in range(HB_B):
        o_ref[:, a * HEAD_DIM:(a + 1) * HEAD_DIM] = (acc_sc[a] / l_sc[a]).astype(BF16)


def _attn_b(proj3, cum3, cum_rows):
    b, s, _ = proj3.shape
    nq = s // TQ_B
    wb = HB_B * HEAD_DIM
    q0 = 3 * WIDTH // wb
    gstep = WIDTH // wb
    return pl.pallas_call(
        _attn_b_kernel,
        name="attn_b",
        grid=(b, N_HEADS // HB_B, nq),
        in_specs=[
            pl.BlockSpec((None, TQ_B, wb), lambda bi, hp, qi: (bi, qi, q0 + hp)),
            pl.BlockSpec((None, s, wb), lambda bi, hp, qi: (bi, 0, q0 + gstep + hp)),
            pl.BlockSpec((None, s, wb), lambda bi, hp, qi: (bi, 0, q0 + 2 * gstep + hp)),
            pl.BlockSpec((None, TQ_B, LANES), lambda bi, hp, qi: (bi, qi, 0)),
            pl.BlockSpec((None, HB_B, s // TK_B, 1, TK_B), lambda bi, hp, qi: (bi, hp, 0, 0, 0)),
        ],
        out_specs=pl.BlockSpec((None, TQ_B, wb), lambda bi, hp, qi: (bi, qi, hp)),
        out_shape=jax.ShapeDtypeStruct((b, s, WIDTH), BF16),
        scratch_shapes=[pltpu.VMEM((HB_B, TQ_B, LANES), F32), pltpu.VMEM((HB_B, TQ_B, LANES), F32),
                        pltpu.VMEM((HB_B, TQ_B, HEAD_DIM), F32)],
        compiler_params=pltpu.CompilerParams(
            dimension_semantics=("arbitrary", "arbitrary", "arbitrary"), vmem_limit_bytes=VMEM_LIMIT),
    )(proj3, proj3, proj3, cum3, cum_rows)


def _merge_kernel(x_ref, ya_ref, yb_ref, ga_ref, gb_ref, wa_ref, wb_ref, wo_ref, fw_ref,
                  wr_hi_ref, wr_lo_ref, br_ref, tri_ref,
                  h_ref, hn_ref, route_ref, cnt_ref, carry_sc):
    i = pl.program_id(0)

    @pl.when(i == 0)
    def _():
        carry_sc[...] = jnp.zeros_like(carry_sc)

    za = _dot(ya_ref[...], wa_ref[...])
    zb = _dot(yb_ref[...], wb_ref[...])
    z = ga_ref[...].astype(F32) * za + gb_ref[...].astype(F32) * zb
    h = x_ref[...] + _dot(z.astype(BF16), wo_ref[...])
    h_ref[...] = h
    ms = jnp.mean(h * h, axis=-1, keepdims=True)
    hn = h * lax.rsqrt(ms + NORM_EPS) * fw_ref[...]
    hn_ref[...] = hn

    logits = _dot_f32x(hn, wr_hi_ref[...], wr_lo_ref[...]) + br_ref[...]
    tm = logits.shape[0]
    lane = lax.broadcasted_iota(jnp.int32, (tm, LANES), 1).astype(F32)
    work = logits
    vals, idxs = [], []
    for _ in range(TOP_K):
        m = jnp.max(work, axis=-1, keepdims=True)
        ix = jnp.min(jnp.where(work == m, lane, float(LANES)), axis=-1, keepdims=True)
        vals.append(m)
        idxs.append(ix)
        work = jnp.where(lane == ix, -jnp.inf, work)
    es = [jnp.exp(v - vals[0]) for v in vals]
    denom = functools.reduce(jnp.add, es)
    onehots = [(lane == ix).astype(F32) for ix in idxs]
    cnt = functools.reduce(jnp.add, onehots)
    before = _dot(tri_ref[...], cnt.astype(BF16)) + carry_sc[...]
    route = jnp.zeros((tm, LANES), F32)
    for k in range(TOP_K):
        rank = jnp.sum(onehots[k] * before, axis=-1, keepdims=True)
        route = jnp.where(lane == float(k), idxs[k], route)
        route = jnp.where(lane == float(TOP_K + k), es[k] / denom, route)
        route = jnp.where(lane == float(2 * TOP_K + k), rank, route)
    route_ref[...] = route
    total = carry_sc[...] + jnp.sum(cnt, axis=0, keepdims=True)
    carry_sc[...] = total
    cnt_ref[...] = total


def _merge(x2, ya, yb, proj, w_a, w_b, w_o, ffn_norm_w, wr_hi, wr_lo, br_pad):
    n, d = x2.shape
    tm = TM_MERGE
    tri = (jnp.arange(tm)[:, None] > jnp.arange(tm)[None, :]).astype(BF16)
    ga_blk = 3 * WIDTH * 2 // d
    const = lambda i: (0, 0)
    return pl.pallas_call(
        _merge_kernel,
        name="merge",
        grid=(n // tm,),
        in_specs=[
            pl.BlockSpec((tm, d), lambda i: (i, 0)),
            pl.BlockSpec((tm, WIDTH), lambda i: (i, 0)),
            pl.BlockSpec((tm, WIDTH), lambda i: (i, 0)),
            pl.BlockSpec((tm, d), lambda i: (i, ga_blk)),
            pl.BlockSpec((tm, d), lambda i: (i, ga_blk + 1)),
            pl.BlockSpec((WIDTH, d), const, pipeline_mode=pl.Buffered(1)),
            pl.BlockSpec((WIDTH, d), const, pipeline_mode=pl.Buffered(1)),
            pl.BlockSpec((d, d), const, pipeline_mode=pl.Buffered(1)),
            pl.BlockSpec((1, d), const),
            pl.BlockSpec((d, LANES), const),
            pl.BlockSpec((d, LANES), const),
            pl.BlockSpec((1, LANES), const),
            pl.BlockSpec((tm, tm), const),
        ],
        out_specs=[
            pl.BlockSpec((tm, d), lambda i: (i, 0)),
            pl.BlockSpec((tm, d), lambda i: (i, 0)),
            pl.BlockSpec((tm, LANES), lambda i: (i, 0)),
            pl.BlockSpec((1, LANES), const),
        ],
        out_shape=[
            jax.ShapeDtypeStruct((n, d), F32),
            jax.ShapeDtypeStruct((n, d), F32),
            jax.ShapeDtypeStruct((n, LANES), F32),
            jax.ShapeDtypeStruct((1, LANES), F32),
        ],
        scratch_shapes=[pltpu.VMEM((1, LANES), F32)],
        compiler_params=pltpu.CompilerParams(
            dimension_semantics=("arbitrary",), vmem_limit_bytes=VMEM_LIMIT),
    )(x2, ya, yb, proj, proj, w_a, w_b, w_o, ffn_norm_w.reshape(1, d), wr_hi, wr_lo, br_pad, tri)


def _dispatch_kernel(cnt_ref, dest_ref, hn_ref, xs_ref, zero_sc, sem, zsem, *, cap):
    i = pl.program_id(0)

    def row_copy(t, k):
        return pltpu.make_async_copy(hn_ref.at[pl.ds(t, 1)], xs_ref.at[pl.ds(dest_ref[t * TOP_K + k], 1)], sem)

    def issue(t, carry):
        for k in range(TOP_K):
            row_copy(t, k).start()
        return carry

    lax.fori_loop(0, TM_ROUTE, issue, 0, unroll=8)

    @pl.when(i == pl.num_programs(0) - 1)
    def _():
        zero_sc[...] = jnp.zeros(zero_sc.shape, F32)

        def zero_fill(e, start):
            first = e * cap + cnt_ref[e]
            aligned = pl.multiple_of((first + SUBLANES - 1) // SUBLANES * SUBLANES, SUBLANES)
            for r in range(SUBLANES - 1):
                row = pltpu.make_async_copy(zero_sc.at[pl.ds(0, 1)], xs_ref.at[pl.ds(first + r, 1)], zsem)
                pl.when(first + r < aligned)(row.start if start else row.wait)
            block = pltpu.make_async_copy(zero_sc, xs_ref.at[pl.ds(aligned, TC_MOE)], zsem)
            block.start() if start else block.wait()

        for e in range(N_EXPERTS):
            zero_fill(e, True)
        for e in range(N_EXPERTS):
            zero_fill(e, False)

    def drain(t, carry):
        for k in range(TOP_K):
            row_copy(t, k).wait()
        return carry

    lax.fori_loop(0, TM_ROUTE, drain, 0, unroll=8)


def _dispatch(counts, dest_flat, hn, cap):
    n, d = hn.shape
    grid_spec = pltpu.PrefetchScalarGridSpec(
        num_scalar_prefetch=1,
        grid=(n // TM_ROUTE,),
        in_specs=[
            pl.BlockSpec((TM_ROUTE * TOP_K,), lambda i, cnt: (i,), memory_space=pltpu.SMEM),
            pl.BlockSpec((TM_ROUTE, d), lambda i, cnt: (i, 0)),
        ],
        out_specs=pl.BlockSpec(memory_space=pl.ANY),
        scratch_shapes=[pltpu.VMEM((TC_MOE, d), F32), pltpu.SemaphoreType.DMA, pltpu.SemaphoreType.DMA],
    )
    return pl.pallas_call(
        functools.partial(_dispatch_kernel, cap=cap),
        name="dispatch",
        grid_spec=grid_spec,
        out_shape=jax.ShapeDtypeStruct((N_EXPERTS * cap, d), F32),
        compiler_params=pltpu.CompilerParams(
            dimension_semantics=("arbitrary",), vmem_limit_bytes=VMEM_LIMIT),
    )(counts, dest_flat, hn)


def _moe_kernel(br_ref, be_ref, nc_ref, nb_ref, x_ref, wg_ref, wu_ref, wd_ref, bg_ref, bu_ref, bd_ref, o_ref,
                x_sc):
    i = pl.program_id(0)
    j = pl.program_id(1)
    d = o_ref.shape[1]

    def process(off, m):
        rows = pl.ds(off, m)

        @pl.when(j == 0)
        def _():
            x_sc[rows, :] = x_ref[rows, :].astype(BF16)
            o_ref[rows, :] = jnp.broadcast_to(bd_ref[...], (m, d))

        x = x_sc[rows, :]
        g = _dot(x, wg_ref[...].astype(BF16)) + bg_ref[...]
        u = _dot(x, wu_ref[...].astype(BF16)) + bu_ref[...]
        g = jnp.minimum(g, SWIGLU_LIMIT)
        u = jnp.clip(u, -SWIGLU_LIMIT, SWIGLU_LIMIT)
        hb = (u + 1.0) * (g * jax.nn.sigmoid(SWIGLU_ALPHA * g))
        o_ref[rows, :] += _dot(hb.astype(BF16), wd_ref[...].astype(BF16))

    @pl.when(i < nb_ref[0])
    def _():
        n_chunks = nc_ref[i]
        pl.when(n_chunks == 4)(lambda: process(0, 4 * TC_MOE))
        pl.when(jnp.logical_or(n_chunks == 2, n_chunks == 3))(lambda: process(0, 2 * TC_MOE))
        pl.when(jnp.logical_or(n_chunks == 1, n_chunks == 3))(
            lambda: process(pl.multiple_of((n_chunks - 1) * TC_MOE, TC_MOE), TC_MOE))


def _moe(xs, blk_row, blk_expert, blk_chunks, n_blocks, w_gate_up, b_gate_up, w_down, b_down, max_blocks):
    rows, d = xs.shape
    assert TM_MOE == 4 * TC_MOE
    nj = D_EXPERT // TF_MOE
    b_gu = b_gate_up.reshape(N_EXPERTS, 1, 2 * D_EXPERT)
    b_dn = b_down.reshape(N_EXPERTS, 1, d)

    def jj(i, j, nb):
        return jnp.where(i < nb[0], j, nj - 1)

    grid_spec = pltpu.PrefetchScalarGridSpec(
        num_scalar_prefetch=4,
        grid=(max_blocks, nj),
        in_specs=[
            pl.BlockSpec((TM_MOE, d), lambda i, j, br, be, nc, nb: (br[i], 0), pipeline_mode=pl.Buffered(1)),
            pl.BlockSpec((None, d, TF_MOE), lambda i, j, br, be, nc, nb: (be[i], 0, jj(i, j, nb))),
            pl.BlockSpec((None, d, TF_MOE), lambda i, j, br, be, nc, nb: (be[i], 0, jj(i, j, nb) + nj)),
            pl.BlockSpec((None, TF_MOE, d), lambda i, j, br, be, nc, nb: (be[i], jj(i, j, nb), 0)),
            pl.BlockSpec((None, 1, TF_MOE), lambda i, j, br, be, nc, nb: (be[i], 0, jj(i, j, nb))),
            pl.BlockSpec((None, 1, TF_MOE), lambda i, j, br, be, nc, nb: (be[i], 0, jj(i, j, nb) + nj)),
            pl.BlockSpec((None, 1, d), lambda i, j, br, be, nc, nb: (be[i], 0, 0)),
        ],
        out_specs=pl.BlockSpec((TM_MOE, d), lambda i, j, br, be, nc, nb: (br[i], 0)),
        scratch_shapes=[pltpu.VMEM((TM_MOE, d), BF16)],
    )
    return pl.pallas_call(
        _moe_kernel,
        name="moe",
        grid_spec=grid_spec,
        out_shape=jax.ShapeDtypeStruct((rows, d), F32),
        compiler_params=pltpu.CompilerParams(
            dimension_semantics=("arbitrary", "arbitrary"), vmem_limit_bytes=VMEM_LIMIT),
    )(blk_row, blk_expert, blk_chunks, n_blocks, xs, w_gate_up, w_gate_up, w_down, b_gu, b_gu, b_dn)


def _combine_kernel(dest_ref, h_ref, route_ref, ys_ref, o_ref, buf, sem):
    def row_copy(t, k):
        return pltpu.make_async_copy(ys_ref.at[pl.ds(dest_ref[t * TOP_K + k], 1)], buf.at[k, pl.ds(t, 1)], sem)

    def issue(t, carry):
        for k in range(TOP_K):
            row_copy(t, k).start()
        return carry

    def drain(t, carry):
        for k in range(TOP_K):
            row_copy(t, k).wait()
        return carry

    lax.fori_loop(0, TM_ROUTE, issue, 0, unroll=8)
    lax.fori_loop(0, TM_ROUTE, drain, 0, unroll=8)
    route = route_ref[...]
    acc = h_ref[...]
    for k in range(TOP_K):
        gate = route[:, TOP_K + k:TOP_K + k + 1]
        acc = acc + gate * buf[k]
    o_ref[...] = acc


def _combine(dest_flat, h, route, ys):
    n, d = h.shape
    return pl.pallas_call(
        _combine_kernel,
        name="combine",
        grid=(n // TM_ROUTE,),
        in_specs=[
            pl.BlockSpec((TM_ROUTE * TOP_K,), lambda i: (i,), memory_space=pltpu.SMEM),
            pl.BlockSpec((TM_ROUTE, d), lambda i: (i, 0)),
            pl.BlockSpec((TM_ROUTE, LANES), lambda i: (i, 0)),
            pl.BlockSpec(memory_space=pl.ANY),
        ],
        out_specs=pl.BlockSpec((TM_ROUTE, d), lambda i: (i, 0)),
        out_shape=jax.ShapeDtypeStruct((n, d), F32),
        scratch_shapes=[pltpu.VMEM((TOP_K, TM_ROUTE, d), F32), pltpu.SemaphoreType.DMA],
        compiler_params=pltpu.CompilerParams(
            dimension_semantics=("arbitrary",), vmem_limit_bytes=VMEM_LIMIT),
    )(dest_flat, h, route, ys)


def _pad_lanes(a):
    return jnp.pad(a, ((0, 0), (0, LANES - a.shape[1])))


def kernel(x, attn_norm_w, w_in, b_forget, qn_a, kn_a, qn_b, kn_b, rel_bias, w_branch_a, w_branch_b,
           w_out, ffn_norm_w, w_router, b_router, w_gate_up, b_gate_up, w_down, b_down):
    b, s, d = x.shape
    n = b * s
    x2 = x.reshape(n, d)

    n_qkv = 6 * WIDTH
    w_all = jnp.concatenate([w_in[:, :n_qkv], w_in[:, n_qkv + N_HEADS:]], axis=1).astype(BF16)
    w_f = _pad_lanes(w_in[:, n_qkv:n_qkv + N_HEADS])
    wf_hi = w_f.astype(BF16)
    wf_lo = (w_f - wf_hi.astype(F32)).astype(BF16)
    bf_pad = _pad_lanes(b_forget.reshape(1, N_HEADS))
    scale = HEAD_DIM ** -0.5
    ones = jnp.ones((HEAD_DIM,), F32)
    head_norm_w = jnp.stack([qn_a * scale, kn_a, ones, qn_b * scale, kn_b, ones, ones, ones])
    wr = _pad_lanes(w_router)
    wr_hi = wr.astype(BF16)
    wr_lo = (wr - wr_hi.astype(F32)).astype(BF16)
    br_pad = jnp.concatenate([b_router.reshape(1, N_EXPERTS),
                              jnp.full((1, LANES - N_EXPERTS), NEG_INF, F32)], axis=1)

    proj, cum = _in_proj(x2, attn_norm_w, w_all, wf_hi, wf_lo, bf_pad, head_norm_w, s)
    proj3 = proj.reshape(b, s, proj.shape[1])

    y_a = _attn_a(proj3, _attn_a_bias(rel_bias))
    cum3 = cum.reshape(b, s, LANES)
    cum_rows = jnp.transpose(cum3[:, :, :N_HEADS], (0, 2, 1)).reshape(b, N_HEADS, s // TK_B, 1, TK_B)
    y_b = _attn_b(proj3, cum3, cum_rows)

    h, hn, route, cnt = _merge(x2, y_a.reshape(n, WIDTH), y_b.reshape(n, WIDTH), proj,
                                w_branch_a.astype(BF16), w_branch_b.astype(BF16), w_out.astype(BF16),
                                ffn_norm_w, wr_hi, wr_lo, br_pad)

    cap = n + TM_MOE
    blocks_per_expert = cap // TM_MOE
    top_e = route[:, :TOP_K].astype(jnp.int32)
    rank = route[:, 2 * TOP_K:3 * TOP_K].astype(jnp.int32)
    dest_flat = (top_e * cap + rank).reshape(n * TOP_K)
    counts = cnt[0, :N_EXPERTS].astype(jnp.int32)
    nblk = (counts + TM_MOE - 1) // TM_MOE
    bend = jnp.cumsum(nblk)
    bstart = bend - nblk
    max_blocks = n * TOP_K // TM_MOE + N_EXPERTS
    blk = jnp.minimum(jnp.arange(max_blocks), bend[-1] - 1)
    blk_expert = jnp.minimum(jnp.sum(blk[:, None] >= bend[None, :], axis=1), N_EXPERTS - 1).astype(jnp.int32)
    blk_in_expert = blk - bstart[blk_expert]
    blk_row = (blk_expert * blocks_per_expert + blk_in_expert).astype(jnp.int32)
    blk_tokens = jnp.clip(counts[blk_expert] - blk_in_expert * TM_MOE, 0, TM_MOE)
    blk_chunks = ((blk_tokens + TC_MOE - 1) // TC_MOE).astype(jnp.int32)
    n_blocks = bend[-1].astype(jnp.int32).reshape(1)

    xs = _dispatch(counts, dest_flat, hn, cap)
    ys = _moe(xs, blk_row, blk_expert, blk_chunks, n_blocks, w_gate_up, b_gate_up, w_down, b_down, max_blocks)
    out = _combine(dest_flat, h, route, ys)
    return out.reshape(b, s, d)
```

```python
import functools

import jax
import jax.numpy as jnp
from jax import lax
from jax.experimental import pallas as pl
from jax.experimental.pallas import tpu as pltpu

D_MODEL = 2048
CHUNK = 64
LEFT_CHUNKS = 8
BAND = (LEFT_CHUNKS + 1) * CHUNK
HEAD_DIM = 128
N_HEADS = 8
WIDTH = N_HEADS * HEAD_DIM
REL_CLIP = 256
N_EXPERTS = 32
TOP_K = 4
D_EXPERT = D_MODEL
SWIGLU_LIMIT = 7.0
SWIGLU_ALPHA = 1.702
NORM_EPS = 1e-5
NEG_INF = -1e30

LANES = 128
SUBLANES = 8
VMEM_LIMIT = 56 * 1024 * 1024

TM_PROJ = 1024
TN_PROJ = 1024
TQ_A = 256
WIN_A = TQ_A + LEFT_CHUNKS * CHUNK
TQ_B = 512
TK_B = 512
HB_B = 4
TM_MERGE = 256
TM_ROUTE = 256
TM_MOE = 1024
TC_MOE = 256
TF_MOE = 256

F32 = jnp.float32
BF16 = jnp.bfloat16


def _dot(a, b):
    return jnp.dot(a, b, preferred_element_type=F32)


def _dot_nt(a, b):
    return lax.dot_general(a, b, (((1,), (1,)), ((), ())), preferred_element_type=F32)


def _split3(x):
    hi = x.astype(BF16)
    r = x - hi.astype(F32)
    mid = r.astype(BF16)
    lo = (r - mid.astype(F32)).astype(BF16)
    return hi, mid, lo


def _dot_f32x(x, w_hi, w_lo):
    hi, mid, _ = _split3(x)
    return _dot(hi, w_hi) + (_dot(hi, w_lo) + _dot(mid, w_hi))


def _in_proj_kernel(x_ref, nw_ref, w_ref, wf_hi_ref, wf_lo_ref, bf_ref, hn_ref, tri_ref,
                    out_ref, cum_ref, xn_sc, carry_sc, *, tiles_per_seq):
    i = pl.program_id(0)
    j = pl.program_id(1)

    @pl.when(j == 0)
    def _():
        x = x_ref[...]
        ms = jnp.mean(x * x, axis=-1, keepdims=True)
        xn = x * lax.rsqrt(ms + NORM_EPS) * nw_ref[...]
        xn_sc[...] = xn.astype(BF16)
        z = _dot_f32x(xn, wf_hi_ref[...], wf_lo_ref[...]) + bf_ref[...]
        logf = jnp.minimum(z, 0.0) - jnp.log1p(jnp.exp(-jnp.abs(z)))
        hi, mid, lo = _split3(logf)
        tri = tri_ref[...]
        c = _dot(tri, hi) + (_dot(tri, mid) + _dot(tri, lo))
        carry = jnp.where(i % tiles_per_seq == 0, 0.0, carry_sc[...])
        c = c + carry
        cum_ref[...] = c
        carry_sc[...] = c[-1:, :]

    acc = _dot(xn_sc[...], w_ref[...])
    is_norm = jnp.logical_and(j < 6, jnp.logical_and(j != 2, j != 5))

    @pl.when(is_norm)
    def _():
        w = hn_ref[pl.ds(j, 1), :]
        for h in range(N_HEADS):
            sl = slice(h * HEAD_DIM, (h + 1) * HEAD_DIM)
            t = acc[:, sl]
            ms = jnp.mean(t * t, axis=-1, keepdims=True)
            out_ref[:, sl] = (t * lax.rsqrt(ms + NORM_EPS) * w).astype(BF16)

    @pl.when(jnp.logical_or(j == 2, j == 5))
    def _():
        out_ref[...] = acc.astype(BF16)

    @pl.when(j >= 6)
    def _():
        out_ref[...] = jax.nn.sigmoid(acc).astype(BF16)


def _in_proj(x2, attn_norm_w, w_all, wf_hi, wf_lo, bf_pad, head_norm_w, seq, tm=TM_PROJ, name="in_proj"):
    n, d = x2.shape
    n_col = w_all.shape[1] // TN_PROJ
    tri = (jnp.arange(tm)[:, None] >= jnp.arange(tm)[None, :]).astype(BF16)
    return pl.pallas_call(
        functools.partial(_in_proj_kernel, tiles_per_seq=seq // tm),
        name=name,
        grid=(n // tm, n_col),
        in_specs=[
            pl.BlockSpec((tm, d), lambda i, j: (i, 0)),
            pl.BlockSpec((1, d), lambda i, j: (0, 0)),
            pl.BlockSpec((d, TN_PROJ), lambda i, j: (0, j)),
            pl.BlockSpec((d, LANES), lambda i, j: (0, 0)),
            pl.BlockSpec((d, LANES), lambda i, j: (0, 0)),
            pl.BlockSpec((1, LANES), lambda i, j: (0, 0)),
            pl.BlockSpec(head_norm_w.shape, lambda i, j: (0, 0)),
            pl.BlockSpec((tm, tm), lambda i, j: (0, 0)),
        ],
        out_specs=[
            pl.BlockSpec((tm, TN_PROJ), lambda i, j: (i, j)),
            pl.BlockSpec((tm, LANES), lambda i, j: (i, 0)),
        ],
        out_shape=[
            jax.ShapeDtypeStruct((n, w_all.shape[1]), BF16),
            jax.ShapeDtypeStruct((n, LANES), F32),
        ],
        scratch_shapes=[pltpu.VMEM((tm, d), BF16), pltpu.VMEM((1, LANES), F32)],
        compiler_params=pltpu.CompilerParams(
            dimension_semantics=("arbitrary", "arbitrary"), vmem_limit_bytes=VMEM_LIMIT),
    )(x2, attn_norm_w.reshape(1, d), w_all, wf_hi, wf_lo, bf_pad, head_norm_w, tri)


def _attn_a_kernel(q_ref, k0_ref, k1_ref, k2_ref, v0_ref, v1_ref, v2_ref, bias_ref, o_ref):
    qb = pl.program_id(1)
    k_refs = (k0_ref, k1_ref, k2_ref)
    v_refs = (v0_ref, v1_ref, v2_ref)
    n_win = len(k_refs)
    for h in range(N_HEADS):
        sl = slice(h * HEAD_DIM, (h + 1) * HEAD_DIM)
        q = q_ref[:, sl]
        scores = []
        for t in range(n_win):
            s = _dot_nt(q, k_refs[t][:, sl]) + bias_ref[h, :, t * TQ_A:(t + 1) * TQ_A]
            if t < n_win - 1:
                s = jnp.where(qb >= n_win - 1 - t, s, NEG_INF)
            scores.append(s)
        m = functools.reduce(jnp.maximum, [jnp.max(s, axis=-1, keepdims=True) for s in scores])
        ps = [jnp.exp(s - m) for s in scores]
        l = functools.reduce(jnp.add, [jnp.sum(p, axis=-1, keepdims=True) for p in ps])
        o = functools.reduce(jnp.add, [_dot(ps[t].astype(BF16), v_refs[t][:, sl]) for t in range(n_win)])
        o_ref[:, sl] = (o / l).astype(BF16)


def _attn_a_bias(rel_bias):
    r = jnp.arange(TQ_A)[:, None]
    c = jnp.arange(WIN_A)[None, :]
    lo = (r // CHUNK) * CHUNK
    in_band = jnp.logical_and(c >= lo, c < lo + BAND)
    rel = r - c + LEFT_CHUNKS * CHUNK
    rel_idx = jnp.clip(rel, -(CHUNK - 1), REL_CLIP) + (CHUNK - 1)
    onehot = (rel_idx[:, :, None] == jnp.arange(rel_bias.shape[1])[None, None, :]).astype(F32)
    bias = jnp.einsum("rct,ht->hrc", onehot, rel_bias.astype(F32), precision=lax.Precision.HIGHEST)
    return jnp.where(in_band[None], bias, NEG_INF)


def _attn_a(proj3, bias):
    b, s, _ = proj3.shape
    nq = s // TQ_A
    n_win = WIN_A // TQ_A

    def kv_spec(group, t):
        return pl.BlockSpec((None, TQ_A, WIDTH),
                            lambda bi, qi: (bi, jnp.maximum(qi - (n_win - 1) + t, 0), group))

    return pl.pallas_call(
        _attn_a_kernel,
        name="attn_a",
        grid=(b, nq),
        in_specs=[pl.BlockSpec((None, TQ_A, WIDTH), lambda bi, qi: (bi, qi, 0))]
        + [kv_spec(1, t) for t in range(n_win)]
        + [kv_spec(2, t) for t in range(n_win)]
        + [pl.BlockSpec(bias.shape, lambda bi, qi: (0, 0, 0))],
        out_specs=pl.BlockSpec((None, TQ_A, WIDTH), lambda bi, qi: (bi, qi, 0)),
        out_shape=jax.ShapeDtypeStruct((b, s, WIDTH), BF16),
        compiler_params=pltpu.CompilerParams(
            dimension_semantics=("arbitrary", "arbitrary"), vmem_limit_bytes=VMEM_LIMIT),
    )(proj3, *([proj3] * (2 * n_win)), bias)


def _attn_b_kernel(q_ref, k_ref, v_ref, cq_ref, ck_ref, o_ref, m_sc, l_sc, acc_sc, *, hb):
    hp = pl.program_id(1)
    qi = pl.program_id(2)
    n_chunk = TK_B // LANES
    lane = lax.broadcasted_iota(jnp.int32, (TQ_B, LANES), 1)
    row = lax.broadcasted_iota(jnp.int32, (TQ_B, LANES), 0)
    qs, cqs = [], []
    for a in range(hb):
        qs.append(q_ref[:, a * HEAD_DIM:(a + 1) * HEAD_DIM])
        cq = jnp.sum(jnp.where(lane == hp * hb + a, cq_ref[...], 0.0), axis=-1, keepdims=True)
        cqs.append(jnp.broadcast_to(cq, (TQ_B, LANES)))
    m_sc[...] = jnp.full(m_sc.shape, NEG_INF, F32)
    l_sc[...] = jnp.zeros(l_sc.shape, F32)
    acc_sc[...] = jnp.zeros(acc_sc.shape, F32)

    def block(kb, masked):
        for a in range(hb):
            sl = slice(a * HEAD_DIM, (a + 1) * HEAD_DIM)
            k = k_ref[pl.ds(kb * TK_B, TK_B), sl]
            v = v_ref[pl.ds(kb * TK_B, TK_B), sl]
            s = _dot_nt(qs[a], k)
            ck = ck_ref[a, kb]
            chunks = []
            for c in range(n_chunk):
                cs = slice(c * LANES, (c + 1) * LANES)
                sc = s[:, cs] + (cqs[a] - ck[:, cs])
                if masked:
                    sc = jnp.where(lane + c * LANES <= row, sc, NEG_INF)
                chunks.append(sc)
            m_old = m_sc[a]
            m_new = jnp.maximum(m_old, jnp.max(functools.reduce(jnp.maximum, chunks), axis=-1, keepdims=True))
            alpha = jnp.exp(m_old - m_new)
            ps = [jnp.exp(sc - m_new) for sc in chunks]
            l_sc[a] = alpha * l_sc[a] + jnp.sum(functools.reduce(jnp.add, ps), axis=-1, keepdims=True)
            p = jnp.concatenate([pc.astype(BF16) for pc in ps], axis=1)
            acc_sc[a] = alpha * acc_sc[a] + _dot(p, v)
            m_sc[a] = m_new

    def body(kb, carry):
        block(kb, False)
        return carry

    lax.fori_loop(0, qi, body, 0)
    block(qi, True)
    for a in range(hb):
        o_ref[:, a * HEAD_DIM:(a + 1) * HEAD_DIM] = (acc_sc[a] / l_sc[a]).astype(BF16)


def _attn_b(proj3, cum3, cum_rows, hb=HB_B, name="attn_b"):
    b, s, _ = proj3.shape
    nq = s // TQ_B
    wb = hb * HEAD_DIM
    q0 = 3 * WIDTH // wb
    gstep = WIDTH // wb
    return pl.pallas_call(
        functools.partial(_attn_b_kernel, hb=hb),
        name=name,
        grid=(b, N_HEADS // hb, nq),
        in_specs=[
            pl.BlockSpec((None, TQ_B, wb), lambda bi, hp, qi: (bi, qi, q0 + hp)),
            pl.BlockSpec((None, s, wb), lambda bi, hp, qi: (bi, 0, q0 + gstep + hp)),
            pl.BlockSpec((None, s, wb), lambda bi, hp, qi: (bi, 0, q0 + 2 * gstep + hp)),
            pl.BlockSpec((None, TQ_B, LANES), lambda bi, hp, qi: (bi, qi, 0)),
            pl.BlockSpec((None, hb, s // TK_B, 1, TK_B), lambda bi, hp, qi: (bi, hp, 0, 0, 0)),
        ],
        out_specs=pl.BlockSpec((None, TQ_B, wb), lambda bi, hp, qi: (bi, qi, hp)),
        out_shape=jax.ShapeDtypeStruct((b, s, WIDTH), BF16),
        scratch_shapes=[pltpu.VMEM((hb, TQ_B, LANES), F32), pltpu.VMEM((hb, TQ_B, LANES), F32),
                        pltpu.VMEM((hb, TQ_B, HEAD_DIM), F32)],
        compiler_params=pltpu.CompilerParams(
            dimension_semantics=("arbitrary", "arbitrary", "arbitrary"), vmem_limit_bytes=VMEM_LIMIT),
    )(proj3, proj3, proj3, cum3, cum_rows)


def _merge_kernel(x_ref, ya_ref, yb_ref, ga_ref, gb_ref, wa_ref, wb_ref, wo_ref, fw_ref,
                  wr_hi_ref, wr_lo_ref, br_ref, tri_ref,
                  h_ref, hn_ref, route_ref, cnt_ref, carry_sc):
    i = pl.program_id(0)

    @pl.when(i == 0)
    def _():
        carry_sc[...] = jnp.zeros_like(carry_sc)

    za = _dot(ya_ref[...], wa_ref[...])
    zb = _dot(yb_ref[...], wb_ref[...])
    z = ga_ref[...].astype(F32) * za + gb_ref[...].astype(F32) * zb
    h = x_ref[...] + _dot(z.astype(BF16), wo_ref[...])
    h_ref[...] = h
    ms = jnp.mean(h * h, axis=-1, keepdims=True)
    hn = h * lax.rsqrt(ms + NORM_EPS) * fw_ref[...]
    hn_ref[...] = hn

    logits = _dot_f32x(hn, wr_hi_ref[...], wr_lo_ref[...]) + br_ref[...]
    tm = logits.shape[0]
    lane = lax.broadcasted_iota(jnp.int32, (tm, LANES), 1).astype(F32)
    work = logits
    vals, idxs = [], []
    for _ in range(TOP_K):
        m = jnp.max(work, axis=-1, keepdims=True)
        ix = jnp.min(jnp.where(work == m, lane, float(LANES)), axis=-1, keepdims=True)
        vals.append(m)
        idxs.append(ix)
        work = jnp.where(lane == ix, -jnp.inf, work)
    es = [jnp.exp(v - vals[0]) for v in vals]
    denom = functools.reduce(jnp.add, es)
    onehots = [(lane == ix).astype(F32) for ix in idxs]
    cnt = functools.reduce(jnp.add, onehots)
    before = _dot(tri_ref[...], cnt.astype(BF16)) + carry_sc[...]
    route = jnp.zeros((tm, LANES), F32)
    for k in range(TOP_K):
        rank = jnp.sum(onehots[k] * before, axis=-1, keepdims=True)
        route = jnp.where(lane == float(k), idxs[k], route)
        route = jnp.where(lane == float(TOP_K + k), es[k] / denom, route)
        route = jnp.where(lane == float(2 * TOP_K + k), rank, route)
    route_ref[...] = route
    total = carry_sc[...] + jnp.sum(cnt, axis=0, keepdims=True)
    carry_sc[...] = total
    cnt_ref[...] = total


def _merge(x2, ya, yb, proj, w_a, w_b, w_o, ffn_norm_w, wr_hi, wr_lo, br_pad):
    n, d = x2.shape
    tm = TM_MERGE
    tri = (jnp.arange(tm)[:, None] > jnp.arange(tm)[None, :]).astype(BF16)
    ga_blk = 3 * WIDTH * 2 // d
    const = lambda i: (0, 0)
    return pl.pallas_call(
        _merge_kernel,
        name="merge",
        grid=(n // tm,),
        in_specs=[
            pl.BlockSpec((tm, d), lambda i: (i, 0)),
            pl.BlockSpec((tm, WIDTH), lambda i: (i, 0)),
            pl.BlockSpec((tm, WIDTH), lambda i: (i, 0)),
            pl.BlockSpec((tm, d), lambda i: (i, ga_blk)),
            pl.BlockSpec((tm, d), lambda i: (i, ga_blk + 1)),
            pl.BlockSpec((WIDTH, d), const, pipeline_mode=pl.Buffered(1)),
            pl.BlockSpec((WIDTH, d), const, pipeline_mode=pl.Buffered(1)),
            pl.BlockSpec((d, d), const, pipeline_mode=pl.Buffered(1)),
            pl.BlockSpec((1, d), const),
            pl.BlockSpec((d, LANES), const),
            pl.BlockSpec((d, LANES), const),
            pl.BlockSpec((1, LANES), const),
            pl.BlockSpec((tm, tm), const),
        ],
        out_specs=[
            pl.BlockSpec((tm, d), lambda i: (i, 0)),
            pl.BlockSpec((tm, d), lambda i: (i, 0)),
            pl.BlockSpec((tm, LANES), lambda i: (i, 0)),
            pl.BlockSpec((1, LANES), const),
        ],
        out_shape=[
            jax.ShapeDtypeStruct((n, d), F32),
            jax.ShapeDtypeStruct((n, d), F32),
            jax.ShapeDtypeStruct((n, LANES), F32),
            jax.ShapeDtypeStruct((1, LANES), F32),
        ],
        scratch_shapes=[pltpu.VMEM((1, LANES), F32)],
        compiler_params=pltpu.CompilerParams(
            dimension_semantics=("arbitrary",), vmem_limit_bytes=VMEM_LIMIT),
    )(x2, ya, yb, proj, proj, w_a, w_b, w_o, ffn_norm_w.reshape(1, d), wr_hi, wr_lo, br_pad, tri)


def _dispatch_kernel(cnt_ref, dest_ref, hn_ref, xs_ref, zero_sc, sem, zsem, *, cap):
    i = pl.program_id(0)

    def row_copy(t, k):
        return pltpu.make_async_copy(hn_ref.at[pl.ds(t, 1)], xs_ref.at[pl.ds(dest_ref[t * TOP_K + k], 1)], sem)

    def issue(t, carry):
        for k in range(TOP_K):
            row_copy(t, k).start()
        return carry

    lax.fori_loop(0, TM_ROUTE, issue, 0, unroll=8)

    @pl.when(i == pl.num_programs(0) - 1)
    def _():
        zero_sc[...] = jnp.zeros(zero_sc.shape, F32)

        def zero_fill(e, start):
            first = e * cap + cnt_ref[e]
            aligned = pl.multiple_of((first + SUBLANES - 1) // SUBLANES * SUBLANES, SUBLANES)
            for r in range(SUBLANES - 1):
                row = pltpu.make_async_copy(zero_sc.at[pl.ds(0, 1)], xs_ref.at[pl.ds(first + r, 1)], zsem)
                pl.when(first + r < aligned)(row.start if start else row.wait)
            block = pltpu.make_async_copy(zero_sc, xs_ref.at[pl.ds(aligned, TC_MOE)], zsem)
            block.start() if start else block.wait()

        for e in range(N_EXPERTS):
            zero_fill(e, True)
        for e in range(N_EXPERTS):
            zero_fill(e, False)

    def drain(t, carry):
        for k in range(TOP_K):
            row_copy(t, k).wait()
        return carry

    lax.fori_loop(0, TM_ROUTE, drain, 0, unroll=8)


def _dispatch(counts, dest_flat, hn, cap):
    n, d = hn.shape
    grid_spec = pltpu.PrefetchScalarGridSpec(
        num_scalar_prefetch=1,
        grid=(n // TM_ROUTE,),
        in_specs=[
            pl.BlockSpec((TM_ROUTE * TOP_K,), lambda i, cnt: (i,), memory_space=pltpu.SMEM),
            pl.BlockSpec((TM_ROUTE, d), lambda i, cnt: (i, 0)),
        ],
        out_specs=pl.BlockSpec(memory_space=pl.ANY),
        scratch_shapes=[pltpu.VMEM((TC_MOE, d), F32), pltpu.SemaphoreType.DMA, pltpu.SemaphoreType.DMA],
    )
    return pl.pallas_call(
        functools.partial(_dispatch_kernel, cap=cap),
        name="dispatch",
        grid_spec=grid_spec,
        out_shape=jax.ShapeDtypeStruct((N_EXPERTS * cap, d), F32),
        compiler_params=pltpu.CompilerParams(
            dimension_semantics=("arbitrary",), vmem_limit_bytes=VMEM_LIMIT),
    )(counts, dest_flat, hn)


def _moe_kernel(br_ref, be_ref, nc_ref, nb_ref, x_ref, wg_ref, wu_ref, wd_ref, bg_ref, bu_ref, bd_ref, o_ref,
                x_sc):
    i = pl.program_id(0)
    j = pl.program_id(1)
    d = o_ref.shape[1]

    def process(off, m):
        rows = pl.ds(off, m)

        @pl.when(j == 0)
        def _():
            x_sc[rows, :] = x_ref[rows, :].astype(BF16)
            o_ref[rows, :] = jnp.broadcast_to(bd_ref[...], (m, d))

        x = x_sc[rows, :]
        g = _dot(x, wg_ref[...].astype(BF16)) + bg_ref[...]
        u = _dot(x, wu_ref[...].astype(BF16)) + bu_ref[...]
        g = jnp.minimum(g, SWIGLU_LIMIT)
        u = jnp.clip(u, -SWIGLU_LIMIT, SWIGLU_LIMIT)
        hb = (u + 1.0) * (g * jax.nn.sigmoid(SWIGLU_ALPHA * g))
        o_ref[rows, :] += _dot(hb.astype(BF16), wd_ref[...].astype(BF16))

    @pl.when(i < nb_ref[0])
    def _():
        n_chunks = nc_ref[i]
        pl.when(n_chunks == 4)(lambda: process(0, 4 * TC_MOE))
        pl.when(jnp.logical_or(n_chunks == 2, n_chunks == 3))(lambda: process(0, 2 * TC_MOE))
        pl.when(jnp.logical_or(n_chunks == 1, n_chunks == 3))(
            lambda: process(pl.multiple_of((n_chunks - 1) * TC_MOE, TC_MOE), TC_MOE))


def _moe(xs, blk_row, blk_expert, blk_chunks, n_blocks, w_gate_up, b_gate_up, w_down, b_down, max_blocks):
    rows, d = xs.shape
    assert TM_MOE == 4 * TC_MOE
    nj = D_EXPERT // TF_MOE
    b_gu = b_gate_up.reshape(N_EXPERTS, 1, 2 * D_EXPERT)
    b_dn = b_down.reshape(N_EXPERTS, 1, d)

    def jj(i, j, nb):
        return jnp.where(i < nb[0], j, nj - 1)

    grid_spec = pltpu.PrefetchScalarGridSpec(
        num_scalar_prefetch=4,
        grid=(max_blocks, nj),
        in_specs=[
            pl.BlockSpec((TM_MOE, d), lambda i, j, br, be, nc, nb: (br[i], 0)),
            pl.BlockSpec((None, d, TF_MOE), lambda i, j, br, be, nc, nb: (be[i], 0, jj(i, j, nb))),
            pl.BlockSpec((None, d, TF_MOE), lambda i, j, br, be, nc, nb: (be[i], 0, jj(i, j, nb) + nj)),
            pl.BlockSpec((None, TF_MOE, d), lambda i, j, br, be, nc, nb: (be[i], jj(i, j, nb), 0)),
            pl.BlockSpec((None, 1, TF_MOE), lambda i, j, br, be, nc, nb: (be[i], 0, jj(i, j, nb))),
            pl.BlockSpec((None, 1, TF_MOE), lambda i, j, br, be, nc, nb: (be[i], 0, jj(i, j, nb) + nj)),
            pl.BlockSpec((None, 1, d), lambda i, j, br, be, nc, nb: (be[i], 0, 0)),
        ],
        out_specs=pl.BlockSpec((TM_MOE, d), lambda i, j, br, be, nc, nb: (br[i], 0)),
        scratch_shapes=[pltpu.VMEM((TM_MOE, d), BF16)],
    )
    return pl.pallas_call(
        _moe_kernel,
        name="moe",
        grid_spec=grid_spec,
        out_shape=jax.ShapeDtypeStruct((rows, d), F32),
        compiler_params=pltpu.CompilerParams(
            dimension_semantics=("arbitrary", "arbitrary"), vmem_limit_bytes=VMEM_LIMIT),
    )(blk_row, blk_expert, blk_chunks, n_blocks, xs, w_gate_up, w_gate_up, w_down, b_gu, b_gu, b_dn)


def _combine_kernel(dest_ref, h_ref, route_ref, ys_ref, o_ref, buf, sem):
    def row_copy(t, k):
        return pltpu.make_async_copy(ys_ref.at[pl.ds(dest_ref[t * TOP_K + k], 1)], buf.at[k, pl.ds(t, 1)], sem)

    def issue(t, carry):
        for k in range(TOP_K):
            row_copy(t, k).start()
        return carry

    def drain(t, carry):
        for k in range(TOP_K):
            row_copy(t, k).wait()
        return carry

    lax.fori_loop(0, TM_ROUTE, issue, 0, unroll=8)
    lax.fori_loop(0, TM_ROUTE, drain, 0, unroll=8)
    route = route_ref[...]
    acc = h_ref[...]
    for k in range(TOP_K):
        gate = route[:, TOP_K + k:TOP_K + k + 1]
        acc = acc + gate * buf[k]
    o_ref[...] = acc


def _combine(dest_flat, h, route, ys):
    n, d = h.shape
    return pl.pallas_call(
        _combine_kernel,
        name="combine",
        grid=(n // TM_ROUTE,),
        in_specs=[
            pl.BlockSpec((TM_ROUTE * TOP_K,), lambda i: (i,), memory_space=pltpu.SMEM),
            pl.BlockSpec((TM_ROUTE, d), lambda i: (i, 0)),
            pl.BlockSpec((TM_ROUTE, LANES), lambda i: (i, 0)),
            pl.BlockSpec(memory_space=pl.ANY),
        ],
        out_specs=pl.BlockSpec((TM_ROUTE, d), lambda i: (i, 0)),
        out_shape=jax.ShapeDtypeStruct((n, d), F32),
        scratch_shapes=[pltpu.VMEM((TOP_K, TM_ROUTE, d), F32), pltpu.SemaphoreType.DMA],
        compiler_params=pltpu.CompilerParams(
            dimension_semantics=("arbitrary",), vmem_limit_bytes=VMEM_LIMIT),
    )(dest_flat, h, route, ys)


def _pad_lanes(a):
    return jnp.pad(a, ((0, 0), (0, LANES - a.shape[1])))


def kernel(x, attn_norm_w, w_in, b_forget, qn_a, kn_a, qn_b, kn_b, rel_bias, w_branch_a, w_branch_b,
           w_out, ffn_norm_w, w_router, b_router, w_gate_up, b_gate_up, w_down, b_down):
    b, s, d = x.shape
    n = b * s
    x2 = x.reshape(n, d)

    n_qkv = 6 * WIDTH
    w_all = jnp.concatenate([w_in[:, :n_qkv], w_in[:, n_qkv + N_HEADS:]], axis=1).astype(BF16)
    w_f = _pad_lanes(w_in[:, n_qkv:n_qkv + N_HEADS])
    wf_hi = w_f.astype(BF16)
    wf_lo = (w_f - wf_hi.astype(F32)).astype(BF16)
    bf_pad = _pad_lanes(b_forget.reshape(1, N_HEADS))
    scale = HEAD_DIM ** -0.5
    ones = jnp.ones((HEAD_DIM,), F32)
    head_norm_w = jnp.stack([qn_a * scale, kn_a, ones, qn_b * scale, kn_b, ones, ones, ones])
    wr = _pad_lanes(w_router)
    wr_hi = wr.astype(BF16)
    wr_lo = (wr - wr_hi.astype(F32)).astype(BF16)
    br_pad = jnp.concatenate([b_router.reshape(1, N_EXPERTS),
                              jnp.full((1, LANES - N_EXPERTS), NEG_INF, F32)], axis=1)

    proj, cum = _in_proj(x2, attn_norm_w, w_all, wf_hi, wf_lo, bf_pad, head_norm_w, s)
    proj3 = proj.reshape(b, s, proj.shape[1])

    y_a = _attn_a(proj3, _attn_a_bias(rel_bias))
    cum3 = cum.reshape(b, s, LANES)
    cum_rows = jnp.transpose(cum3[:, :, :N_HEADS], (0, 2, 1)).reshape(b, N_HEADS, s // TK_B, 1, TK_B)
    y_b = _attn_b(proj3, cum3, cum_rows)

    h, hn, route, cnt = _merge(x2, y_a.reshape(n, WIDTH), y_b.reshape(n, WIDTH), proj,
                                w_branch_a.astype(BF16), w_branch_b.astype(BF16), w_out.astype(BF16),
                                ffn_norm_w, wr_hi, wr_lo, br_pad)

    cap = n + TM_MOE
    blocks_per_expert = cap // TM_MOE
    top_e = route[:, :TOP_K].astype(jnp.int32)
    rank = route[:, 2 * TOP_K:3 * TOP_K].astype(jnp.int32)
    dest_flat = (top_e * cap + rank).reshape(n * TOP_K)
    counts = cnt[0, :N_EXPERTS].astype(jnp.int32)
    nblk = (counts + TM_MOE - 1) // TM_MOE
    bend = jnp.cumsum(nblk)
    bstart = bend - nblk
    max_blocks = n * TOP_K // TM_MOE + N_EXPERTS
    blk = jnp.minimum(jnp.arange(max_blocks), bend[-1] - 1)
    blk_expert = jnp.minimum(jnp.sum(blk[:, None] >= bend[None, :], axis=1), N_EXPERTS - 1).astype(jnp.int32)
    blk_in_expert = blk - bstart[blk_expert]
    blk_row = (blk_expert * blocks_per_expert + blk_in_expert).astype(jnp.int32)
    blk_tokens = jnp.clip(counts[blk_expert] - blk_in_expert * TM_MOE, 0, TM_MOE)
    blk_chunks = ((blk_tokens + TC_MOE - 1) // TC_MOE).astype(jnp.int32)
    n_blocks = bend[-1].astype(jnp.int32).reshape(1)

    xs = _dispatch(counts, dest_flat, hn, cap)
    ys = _moe(xs, blk_row, blk_expert, blk_chunks, n_blocks, w_gate_up, b_gate_up, w_down, b_down, max_blocks)
    out = _combine(dest_flat, h, route, ys)
    return out.reshape(b, s, d)
```

```python
import functools

import jax
import jax.numpy as jnp
from jax import lax
from jax.experimental import pallas as pl
from jax.experimental.pallas import tpu as pltpu

D_MODEL = 2048
CHUNK = 64
LEFT_CHUNKS = 8
BAND = (LEFT_CHUNKS + 1) * CHUNK
HEAD_DIM = 128
N_HEADS = 8
WIDTH = N_HEADS * HEAD_DIM
REL_CLIP = 256
N_EXPERTS = 32
TOP_K = 4
D_EXPERT = D_MODEL
SWIGLU_LIMIT = 7.0
SWIGLU_ALPHA = 1.702
NORM_EPS = 1e-5
NEG_INF = -1e30

LANES = 128
SUBLANES = 8
VMEM_LIMIT = 56 * 1024 * 1024

TM_PROJ = 1024
TN_PROJ = 1024
TQ_A = 256
WIN_A = TQ_A + LEFT_CHUNKS * CHUNK
TQ_B = 512
TK_B = 512
HB_B = 4
TM_MERGE = 256
TM_ROUTE = 256
TM_MOE = 1024
TC_MOE = 256
TF_MOE = 256

F32 = jnp.float32
BF16 = jnp.bfloat16
U32 = jnp.uint32


def _dot(a, b):
    return jnp.dot(a, b, preferred_element_type=F32)


def _dot_nt(a, b):
    return lax.dot_general(a, b, (((1,), (1,)), ((), ())), preferred_element_type=F32)


def _pack_bf16_pair(lo, hi):
    lo_bits = lax.bitcast_convert_type(lo.astype(BF16).astype(F32), U32)
    hi_bits = lax.bitcast_convert_type(hi.astype(BF16).astype(F32), U32)
    return (lo_bits >> 16) | hi_bits


def _unpack_bf16_pair(w):
    lo = lax.bitcast_convert_type(w << 16, F32)
    hi = lax.bitcast_convert_type(w & jnp.uint32(0xFFFF0000), F32)
    return lo, hi


def _split3(x):
    hi = x.astype(BF16)
    r = x - hi.astype(F32)
    mid = r.astype(BF16)
    lo = (r - mid.astype(F32)).astype(BF16)
    return hi, mid, lo


def _dot_f32x(x, w_hi, w_lo):
    hi, mid, _ = _split3(x)
    return _dot(hi, w_hi) + (_dot(hi, w_lo) + _dot(mid, w_hi))


def _in_proj_kernel(x_ref, nw_ref, w_ref, wf_hi_ref, wf_lo_ref, bf_ref, hn_ref, tri_ref,
                    out_ref, cum_ref, xn_sc, carry_sc, *, tiles_per_seq):
    i = pl.program_id(0)
    j = pl.program_id(1)

    @pl.when(j == 0)
    def _():
        x = x_ref[...]
        ms = jnp.mean(x * x, axis=-1, keepdims=True)
        xn = x * lax.rsqrt(ms + NORM_EPS) * nw_ref[...]
        xn_sc[...] = xn.astype(BF16)
        z = _dot_f32x(xn, wf_hi_ref[...], wf_lo_ref[...]) + bf_ref[...]
        logf = jnp.minimum(z, 0.0) - jnp.log1p(jnp.exp(-jnp.abs(z)))
        hi, mid, lo = _split3(logf)
        tri = tri_ref[...]
        c = _dot(tri, hi) + (_dot(tri, mid) + _dot(tri, lo))
        carry = jnp.where(i % tiles_per_seq == 0, 0.0, carry_sc[...])
        c = c + carry
        cum_ref[...] = c
        carry_sc[...] = c[-1:, :]

    acc = _dot(xn_sc[...], w_ref[...])
    is_norm = jnp.logical_and(j < 6, jnp.logical_and(j != 2, j != 5))

    @pl.when(is_norm)
    def _():
        w = hn_ref[pl.ds(j, 1), :]
        for h in range(N_HEADS):
            sl = slice(h * HEAD_DIM, (h + 1) * HEAD_DIM)
            t = acc[:, sl]
            ms = jnp.mean(t * t, axis=-1, keepdims=True)
            out_ref[:, sl] = (t * lax.rsqrt(ms + NORM_EPS) * w).astype(BF16)

    @pl.when(jnp.logical_or(j == 2, j == 5))
    def _():
        out_ref[...] = acc.astype(BF16)

    @pl.when(j >= 6)
    def _():
        out_ref[...] = jax.nn.sigmoid(acc).astype(BF16)


def _in_proj(x2, attn_norm_w, w_all, wf_hi, wf_lo, bf_pad, head_norm_w, seq, tm=TM_PROJ, name="in_proj"):
    n, d = x2.shape
    n_col = w_all.shape[1] // TN_PROJ
    tri = (jnp.arange(tm)[:, None] >= jnp.arange(tm)[None, :]).astype(BF16)
    return pl.pallas_call(
        functools.partial(_in_proj_kernel, tiles_per_seq=seq // tm),
        name=name,
        grid=(n // tm, n_col),
        in_specs=[
            pl.BlockSpec((tm, d), lambda i, j: (i, 0)),
            pl.BlockSpec((1, d), lambda i, j: (0, 0)),
            pl.BlockSpec((d, TN_PROJ), lambda i, j: (0, j)),
            pl.BlockSpec((d, LANES), lambda i, j: (0, 0)),
            pl.BlockSpec((d, LANES), lambda i, j: (0, 0)),
            pl.BlockSpec((1, LANES), lambda i, j: (0, 0)),
            pl.BlockSpec(head_norm_w.shape, lambda i, j: (0, 0)),
            pl.BlockSpec((tm, tm), lambda i, j: (0, 0)),
        ],
        out_specs=[
            pl.BlockSpec((tm, TN_PROJ), lambda i, j: (i, j)),
            pl.BlockSpec((tm, LANES), lambda i, j: (i, 0)),
        ],
        out_shape=[
            jax.ShapeDtypeStruct((n, w_all.shape[1]), BF16),
            jax.ShapeDtypeStruct((n, LANES), F32),
        ],
        scratch_shapes=[pltpu.VMEM((tm, d), BF16), pltpu.VMEM((1, LANES), F32)],
        compiler_params=pltpu.CompilerParams(
            dimension_semantics=("arbitrary", "arbitrary"), vmem_limit_bytes=VMEM_LIMIT),
    )(x2, attn_norm_w.reshape(1, d), w_all, wf_hi, wf_lo, bf_pad, head_norm_w, tri)


def _attn_a_kernel(q_ref, k0_ref, k1_ref, k2_ref, v0_ref, v1_ref, v2_ref, bias_ref, o_ref):
    qb = pl.program_id(1)
    k_refs = (k0_ref, k1_ref, k2_ref)
    v_refs = (v0_ref, v1_ref, v2_ref)
    n_win = len(k_refs)
    for h in range(N_HEADS):
        sl = slice(h * HEAD_DIM, (h + 1) * HEAD_DIM)
        q = q_ref[:, sl]
        scores = []
        for t in range(n_win):
            s = _dot_nt(q, k_refs[t][:, sl]) + bias_ref[h, :, t * TQ_A:(t + 1) * TQ_A]
            if t < n_win - 1:
                s = jnp.where(qb >= n_win - 1 - t, s, NEG_INF)
            scores.append(s)
        m = functools.reduce(jnp.maximum, [jnp.max(s, axis=-1, keepdims=True) for s in scores])
        ps = [jnp.exp(s - m) for s in scores]
        l = functools.reduce(jnp.add, [jnp.sum(p, axis=-1, keepdims=True) for p in ps])
        o = functools.reduce(jnp.add, [_dot(ps[t].astype(BF16), v_refs[t][:, sl]) for t in range(n_win)])
        o_ref[:, sl] = (o / l).astype(BF16)


def _attn_a_bias(rel_bias):
    r = jnp.arange(TQ_A)[:, None]
    c = jnp.arange(WIN_A)[None, :]
    lo = (r // CHUNK) * CHUNK
    in_band = jnp.logical_and(c >= lo, c < lo + BAND)
    rel = r - c + LEFT_CHUNKS * CHUNK
    rel_idx = jnp.clip(rel, -(CHUNK - 1), REL_CLIP) + (CHUNK - 1)
    del rel_idx
    j = jnp.arange(TQ_A - 1 + WIN_A)
    ext_rel = (TQ_A - 1 + LEFT_CHUNKS * CHUNK) - j
    ext = rel_bias[:, jnp.clip(ext_rel, -(CHUNK - 1), REL_CLIP) + (CHUNK - 1)].astype(F32)
    starts = TQ_A - 1 - jnp.arange(TQ_A)
    bias = jax.vmap(lambda st: lax.dynamic_slice_in_dim(ext, st, WIN_A, axis=1), out_axes=1)(starts)
    return jnp.where(in_band[None], bias, NEG_INF)


def _attn_a(proj3, bias):
    b, s, _ = proj3.shape
    nq = s // TQ_A
    n_win = WIN_A // TQ_A

    def kv_spec(group, t):
        return pl.BlockSpec((None, TQ_A, WIDTH),
                            lambda bi, qi: (bi, jnp.maximum(qi - (n_win - 1) + t, 0), group))

    return pl.pallas_call(
        _attn_a_kernel,
        name="attn_a",
        grid=(b, nq),
        in_specs=[pl.BlockSpec((None, TQ_A, WIDTH), lambda bi, qi: (bi, qi, 0))]
        + [kv_spec(1, t) for t in range(n_win)]
        + [kv_spec(2, t) for t in range(n_win)]
        + [pl.BlockSpec(bias.shape, lambda bi, qi: (0, 0, 0))],
        out_specs=pl.BlockSpec((None, TQ_A, WIDTH), lambda bi, qi: (bi, qi, 0)),
        out_shape=jax.ShapeDtypeStruct((b, s, WIDTH), BF16),
        compiler_params=pltpu.CompilerParams(
            dimension_semantics=("arbitrary", "arbitrary"), vmem_limit_bytes=VMEM_LIMIT),
    )(proj3, *([proj3] * (2 * n_win)), bias)


def _attn_b_kernel(q_ref, k_ref, v_ref, cq_ref, ck_ref, o_ref, m_sc, l_sc, acc_sc, *, hb):
    hp = pl.program_id(1)
    qi = pl.program_id(2)
    n_chunk = TK_B // LANES
    lane = lax.broadcasted_iota(jnp.int32, (TQ_B, LANES), 1)
    row = lax.broadcasted_iota(jnp.int32, (TQ_B, LANES), 0)
    qs, cqs = [], []
    for a in range(hb):
        qs.append(q_ref[:, a * HEAD_DIM:(a + 1) * HEAD_DIM])
        cq = jnp.sum(jnp.where(lane == hp * hb + a, cq_ref[...], 0.0), axis=-1, keepdims=True)
        cqs.append(jnp.broadcast_to(cq, (TQ_B, LANES)))
    m_sc[...] = jnp.full(m_sc.shape, NEG_INF, F32)
    l_sc[...] = jnp.zeros(l_sc.shape, F32)
    acc_sc[...] = jnp.zeros(acc_sc.shape, F32)

    def block(kb, masked):
        for a in range(hb):
            sl = slice(a * HEAD_DIM, (a + 1) * HEAD_DIM)
            k = k_ref[pl.ds(kb * TK_B, TK_B), sl]
            v = v_ref[pl.ds(kb * TK_B, TK_B), sl]
            s = _dot_nt(qs[a], k)
            ck = ck_ref[a, kb]
            chunks = []
            for c in range(n_chunk):
                cs = slice(c * LANES, (c + 1) * LANES)
                sc = s[:, cs] + (cqs[a] - ck[:, cs])
                if masked:
                    sc = jnp.where(lane + c * LANES <= row, sc, NEG_INF)
                chunks.append(sc)
            m_old = m_sc[a]
            m_new = jnp.maximum(m_old, jnp.max(functools.reduce(jnp.maximum, chunks), axis=-1, keepdims=True))
            alpha = jnp.exp(m_old - m_new)
            ps = [jnp.exp(sc - m_new) for sc in chunks]
            l_sc[a] = alpha * l_sc[a] + jnp.sum(functools.reduce(jnp.add, ps), axis=-1, keepdims=True)
            p = jnp.concatenate([pc.astype(BF16) for pc in ps], axis=1)
            acc_sc[a] = alpha * acc_sc[a] + _dot(p, v)
            m_sc[a] = m_new

    def body(kb, carry):
        block(kb, False)
        return carry

    lax.fori_loop(0, qi, body, 0)
    block(qi, True)
    for a in range(hb):
        o_ref[:, a * HEAD_DIM:(a + 1) * HEAD_DIM] = (acc_sc[a] / l_sc[a]).astype(BF16)


def _attn_b(proj3, cum3, cum_rows, hb=HB_B, name="attn_b"):
    b, s, _ = proj3.shape
    nq = s // TQ_B
    wb = hb * HEAD_DIM
    q0 = 3 * WIDTH // wb
    gstep = WIDTH // wb
    return pl.pallas_call(
        functools.partial(_attn_b_kernel, hb=hb),
        name=name,
        grid=(b, N_HEADS // hb, nq),
        in_specs=[
            pl.BlockSpec((None, TQ_B, wb), lambda bi, hp, qi: (bi, qi, q0 + hp)),
            pl.BlockSpec((None, s, wb), lambda bi, hp, qi: (bi, 0, q0 + gstep + hp)),
            pl.BlockSpec((None, s, wb), lambda bi, hp, qi: (bi, 0, q0 + 2 * gstep + hp)),
            pl.BlockSpec((None, TQ_B, LANES), lambda bi, hp, qi: (bi, qi, 0)),
            pl.BlockSpec((None, hb, s // TK_B, 1, TK_B), lambda bi, hp, qi: (bi, hp, 0, 0, 0)),
        ],
        out_specs=pl.BlockSpec((None, TQ_B, wb), lambda bi, hp, qi: (bi, qi, hp)),
        out_shape=jax.ShapeDtypeStruct((b, s, WIDTH), BF16),
        scratch_shapes=[pltpu.VMEM((hb, TQ_B, LANES), F32), pltpu.VMEM((hb, TQ_B, LANES), F32),
                        pltpu.VMEM((hb, TQ_B, HEAD_DIM), F32)],
        compiler_params=pltpu.CompilerParams(
            dimension_semantics=("arbitrary", "arbitrary", "arbitrary"), vmem_limit_bytes=VMEM_LIMIT),
    )(proj3, proj3, proj3, cum3, cum_rows)


def _merge_kernel(x_ref, ya_ref, yb_ref, ga_ref, gb_ref, wa_ref, wb_ref, wo_ref, fw_ref,
                  wr_hi_ref, wr_lo_ref, br_ref, tri_ref,
                  h_ref, hn_ref, route_ref, cnt_ref, carry_sc):
    i = pl.program_id(0)

    @pl.when(i == 0)
    def _():
        carry_sc[...] = jnp.zeros_like(carry_sc)

    za = _dot(ya_ref[...], wa_ref[...])
    zb = _dot(yb_ref[...], wb_ref[...])
    z = ga_ref[...].astype(F32) * za + gb_ref[...].astype(F32) * zb
    h = x_ref[...] + _dot(z.astype(BF16), wo_ref[...])
    h_ref[...] = h
    ms = jnp.mean(h * h, axis=-1, keepdims=True)
    hn = h * lax.rsqrt(ms + NORM_EPS) * fw_ref[...]
    half = hn.shape[1] // 2
    hn_ref[...] = _pack_bf16_pair(hn[:, :half], hn[:, half:])

    logits = _dot_f32x(hn, wr_hi_ref[...], wr_lo_ref[...]) + br_ref[...]
    tm = logits.shape[0]
    lane = lax.broadcasted_iota(jnp.int32, (tm, LANES), 1).astype(F32)
    work = logits
    vals, idxs = [], []
    for _ in range(TOP_K):
        m = jnp.max(work, axis=-1, keepdims=True)
        ix = jnp.min(jnp.where(work == m, lane, float(LANES)), axis=-1, keepdims=True)
        vals.append(m)
        idxs.append(ix)
        work = jnp.where(lane == ix, -jnp.inf, work)
    es = [jnp.exp(v - vals[0]) for v in vals]
    denom = functools.reduce(jnp.add, es)
    onehots = [(lane == ix).astype(F32) for ix in idxs]
    cnt = functools.reduce(jnp.add, onehots)
    before = _dot(tri_ref[...], cnt.astype(BF16)) + carry_sc[...]
    route = jnp.zeros((tm, LANES), F32)
    for k in range(TOP_K):
        rank = jnp.sum(onehots[k] * before, axis=-1, keepdims=True)
        route = jnp.where(lane == float(k), idxs[k], route)
        route = jnp.where(lane == float(TOP_K + k), es[k] / denom, route)
        route = jnp.where(lane == float(2 * TOP_K + k), rank, route)
    route_ref[...] = route
    total = carry_sc[...] + jnp.sum(cnt, axis=0, keepdims=True)
    carry_sc[...] = total
    cnt_ref[...] = total


def _merge(x2, ya, yb, proj, w_a, w_b, w_o, ffn_norm_w, wr_hi, wr_lo, br_pad):
    n, d = x2.shape
    tm = TM_MERGE
    tri = (jnp.arange(tm)[:, None] > jnp.arange(tm)[None, :]).astype(BF16)
    ga_blk = 3 * WIDTH * 2 // d
    const = lambda i: (0, 0)
    return pl.pallas_call(
        _merge_kernel,
        name="merge",
        grid=(n // tm,),
        in_specs=[
            pl.BlockSpec((tm, d), lambda i: (i, 0)),
            pl.BlockSpec((tm, WIDTH), lambda i: (i, 0)),
            pl.BlockSpec((tm, WIDTH), lambda i: (i, 0)),
            pl.BlockSpec((tm, d), lambda i: (i, ga_blk)),
            pl.BlockSpec((tm, d), lambda i: (i, ga_blk + 1)),
            pl.BlockSpec((WIDTH, d), const, pipeline_mode=pl.Buffered(1)),
            pl.BlockSpec((WIDTH, d), const, pipeline_mode=pl.Buffered(1)),
            pl.BlockSpec((d, d), const, pipeline_mode=pl.Buffered(1)),
            pl.BlockSpec((1, d), const),
            pl.BlockSpec((d, LANES), const),
            pl.BlockSpec((d, LANES), const),
            pl.BlockSpec((1, LANES), const),
            pl.BlockSpec((tm, tm), const),
        ],
        out_specs=[
            pl.BlockSpec((tm, d), lambda i: (i, 0)),
            pl.BlockSpec((tm, d // 2), lambda i: (i, 0)),
            pl.BlockSpec((tm, LANES), lambda i: (i, 0)),
            pl.BlockSpec((1, LANES), const),
        ],
        out_shape=[
            jax.ShapeDtypeStruct((n, d), F32),
            jax.ShapeDtypeStruct((n, d // 2), U32),
            jax.ShapeDtypeStruct((n, LANES), F32),
            jax.ShapeDtypeStruct((1, LANES), F32),
        ],
        scratch_shapes=[pltpu.VMEM((1, LANES), F32)],
        compiler_params=pltpu.CompilerParams(
            dimension_semantics=("arbitrary",), vmem_limit_bytes=VMEM_LIMIT),
    )(x2, ya, yb, proj, proj, w_a, w_b, w_o, ffn_norm_w.reshape(1, d), wr_hi, wr_lo, br_pad, tri)


def _dispatch_kernel(cnt_ref, dest_ref, hn_ref, xs_ref, zero_sc, sem, zsem, *, cap):
    i = pl.program_id(0)

    def row_copy(t, k):
        return pltpu.make_async_copy(hn_ref.at[pl.ds(t, 1)], xs_ref.at[pl.ds(dest_ref[t * TOP_K + k], 1)], sem)

    def issue(t, carry):
        for k in range(TOP_K):
            row_copy(t, k).start()
        return carry

    lax.fori_loop(0, TM_ROUTE, issue, 0, unroll=8)

    @pl.when(i == pl.num_programs(0) - 1)
    def _():
        zero_sc[...] = jnp.zeros(zero_sc.shape, zero_sc.dtype)

        def zero_fill(e, start):
            first = e * cap + cnt_ref[e]
            aligned = pl.multiple_of((first + SUBLANES - 1) // SUBLANES * SUBLANES, SUBLANES)
            for r in range(SUBLANES - 1):
                row = pltpu.make_async_copy(zero_sc.at[pl.ds(0, 1)], xs_ref.at[pl.ds(first + r, 1)], zsem)
                pl.when(first + r < aligned)(row.start if start else row.wait)
            block = pltpu.make_async_copy(zero_sc, xs_ref.at[pl.ds(aligned, TC_MOE)], zsem)
            block.start() if start else block.wait()

        for e in range(N_EXPERTS):
            zero_fill(e, True)
        for e in range(N_EXPERTS):
            zero_fill(e, False)

    def drain(t, carry):
        for k in range(TOP_K):
            row_copy(t, k).wait()
        return carry

    lax.fori_loop(0, TM_ROUTE, drain, 0, unroll=8)


def _dispatch(counts, dest_flat, hn, cap):
    n, d = hn.shape
    grid_spec = pltpu.PrefetchScalarGridSpec(
        num_scalar_prefetch=1,
        grid=(n // TM_ROUTE,),
        in_specs=[
            pl.BlockSpec((TM_ROUTE * TOP_K,), lambda i, cnt: (i,), memory_space=pltpu.SMEM),
            pl.BlockSpec((TM_ROUTE, d), lambda i, cnt: (i, 0)),
        ],
        out_specs=pl.BlockSpec(memory_space=pl.ANY),
        scratch_shapes=[pltpu.VMEM((TC_MOE, d), hn.dtype), pltpu.SemaphoreType.DMA, pltpu.SemaphoreType.DMA],
    )
    return pl.pallas_call(
        functools.partial(_dispatch_kernel, cap=cap),
        name="dispatch",
        grid_spec=grid_spec,
        out_shape=jax.ShapeDtypeStruct((N_EXPERTS * cap, d), hn.dtype),
        compiler_params=pltpu.CompilerParams(
            dimension_semantics=("arbitrary",), vmem_limit_bytes=VMEM_LIMIT),
    )(counts, dest_flat, hn)


def _moe_kernel(br_ref, be_ref, nc_ref, nb_ref, x_ref, wg_ref, wu_ref, wd_ref, bg_ref, bu_ref, bd_ref, o_ref,
                x_sc, acc_sc):
    i = pl.program_id(0)
    j = pl.program_id(1)
    d = acc_sc.shape[1]
    half = d // 2

    def process(off, m):
        rows = pl.ds(off, m)

        @pl.when(j == 0)
        def _():
            lo, hi = _unpack_bf16_pair(x_ref[rows, :])
            x_sc[rows, :half] = lo.astype(BF16)
            x_sc[rows, half:] = hi.astype(BF16)
            acc_sc[rows, :] = jnp.broadcast_to(bd_ref[...], (m, d))

        x = x_sc[rows, :]
        g = _dot(x, wg_ref[...].astype(BF16)) + bg_ref[...]
        u = _dot(x, wu_ref[...].astype(BF16)) + bu_ref[...]
        g = jnp.minimum(g, SWIGLU_LIMIT)
        u = jnp.clip(u, -SWIGLU_LIMIT, SWIGLU_LIMIT)
        hb = (u + 1.0) * (g * jax.nn.sigmoid(SWIGLU_ALPHA * g))
        acc_sc[rows, :] += _dot(hb.astype(BF16), wd_ref[...].astype(BF16))

        @pl.when(j == pl.num_programs(1) - 1)
        def _():
            y = acc_sc[rows, :]
            o_ref[rows, :] = _pack_bf16_pair(y[:, :half], y[:, half:])

    @pl.when(i < nb_ref[0])
    def _():
        n_chunks = nc_ref[i]
        pl.when(n_chunks == 4)(lambda: process(0, 4 * TC_MOE))
        pl.when(jnp.logical_or(n_chunks == 2, n_chunks == 3))(lambda: process(0, 2 * TC_MOE))
        pl.when(jnp.logical_or(n_chunks == 1, n_chunks == 3))(
            lambda: process(pl.multiple_of((n_chunks - 1) * TC_MOE, TC_MOE), TC_MOE))


def _moe(xs, blk_row, blk_expert, blk_chunks, n_blocks, w_gate_up, b_gate_up, w_down, b_down, max_blocks):
    rows, half = xs.shape
    d = 2 * half
    assert TM_MOE == 4 * TC_MOE
    nj = D_EXPERT // TF_MOE
    b_gu = b_gate_up.reshape(N_EXPERTS, 1, 2 * D_EXPERT)
    b_dn = b_down.reshape(N_EXPERTS, 1, d)

    def jj(i, j, nb):
        return jnp.where(i < nb[0], j, nj - 1)

    grid_spec = pltpu.PrefetchScalarGridSpec(
        num_scalar_prefetch=4,
        grid=(max_blocks, nj),
        in_specs=[
            pl.BlockSpec((TM_MOE, half), lambda i, j, br, be, nc, nb: (br[i], 0)),
            pl.BlockSpec((None, d, TF_MOE), lambda i, j, br, be, nc, nb: (be[i], 0, jj(i, j, nb))),
            pl.BlockSpec((None, d, TF_MOE), lambda i, j, br, be, nc, nb: (be[i], 0, jj(i, j, nb) + nj)),
            pl.BlockSpec((None, TF_MOE, d), lambda i, j, br, be, nc, nb: (be[i], jj(i, j, nb), 0)),
            pl.BlockSpec((None, 1, TF_MOE), lambda i, j, br, be, nc, nb: (be[i], 0, jj(i, j, nb))),
            pl.BlockSpec((None, 1, TF_MOE), lambda i, j, br, be, nc, nb: (be[i], 0, jj(i, j, nb) + nj)),
            pl.BlockSpec((None, 1, d), lambda i, j, br, be, nc, nb: (be[i], 0, 0)),
        ],
        out_specs=pl.BlockSpec((TM_MOE, half), lambda i, j, br, be, nc, nb: (br[i], 0)),
        scratch_shapes=[pltpu.VMEM((TM_MOE, d), BF16), pltpu.VMEM((TM_MOE, d), F32)],
    )
    return pl.pallas_call(
        _moe_kernel,
        name="moe",
        grid_spec=grid_spec,
        out_shape=jax.ShapeDtypeStruct((rows, half), U32),
        compiler_params=pltpu.CompilerParams(
            dimension_semantics=("arbitrary", "arbitrary"), vmem_limit_bytes=VMEM_LIMIT),
    )(blk_row, blk_expert, blk_chunks, n_blocks, xs, w_gate_up, w_gate_up, w_down, b_gu, b_gu, b_dn)


def _combine_kernel(dest_ref, h_ref, route_ref, ys_ref, o_ref, buf, sem):
    def row_copy(t, k):
        return pltpu.make_async_copy(ys_ref.at[pl.ds(dest_ref[t * TOP_K + k], 1)], buf.at[k, pl.ds(t, 1)], sem)

    def issue(t, carry):
        for k in range(TOP_K):
            row_copy(t, k).start()
        return carry

    def drain(t, carry):
        for k in range(TOP_K):
            row_copy(t, k).wait()
        return carry

    lax.fori_loop(0, TM_ROUTE, issue, 0, unroll=8)
    lax.fori_loop(0, TM_ROUTE, drain, 0, unroll=8)
    route = route_ref[...]
    half = buf.shape[2]
    acc_lo = h_ref[:, :half]
    acc_hi = h_ref[:, half:]
    for k in range(TOP_K):
        gate = route[:, TOP_K + k:TOP_K + k + 1]
        lo, hi = _unpack_bf16_pair(buf[k])
        acc_lo = acc_lo + gate * lo
        acc_hi = acc_hi + gate * hi
    o_ref[:, :half] = acc_lo
    o_ref[:, half:] = acc_hi


def _combine(dest_flat, h, route, ys):
    n, d = h.shape
    return pl.pallas_call(
        _combine_kernel,
        name="combine",
        grid=(n // TM_ROUTE,),
        in_specs=[
            pl.BlockSpec((TM_ROUTE * TOP_K,), lambda i: (i,), memory_space=pltpu.SMEM),
            pl.BlockSpec((TM_ROUTE, d), lambda i: (i, 0)),
            pl.BlockSpec((TM_ROUTE, LANES), lambda i: (i, 0)),
            pl.BlockSpec(memory_space=pl.ANY),
        ],
        out_specs=pl.BlockSpec((TM_ROUTE, d), lambda i: (i, 0)),
        out_shape=jax.ShapeDtypeStruct((n, d), F32),
        scratch_shapes=[pltpu.VMEM((TOP_K, TM_ROUTE, d // 2), U32), pltpu.SemaphoreType.DMA],
        compiler_params=pltpu.CompilerParams(
            dimension_semantics=("arbitrary",), vmem_limit_bytes=VMEM_LIMIT),
    )(dest_flat, h, route, ys)


def _pad_lanes(a):
    return jnp.pad(a, ((0, 0), (0, LANES - a.shape[1])))


def kernel(x, attn_norm_w, w_in, b_forget, qn_a, kn_a, qn_b, kn_b, rel_bias, w_branch_a, w_branch_b,
           w_out, ffn_norm_w, w_router, b_router, w_gate_up, b_gate_up, w_down, b_down):
    b, s, d = x.shape
    n = b * s
    x2 = x.reshape(n, d)

    n_qkv = 6 * WIDTH
    w_all = jnp.concatenate([w_in[:, :n_qkv], w_in[:, n_qkv + N_HEADS:]], axis=1).astype(BF16)
    w_f = _pad_lanes(w_in[:, n_qkv:n_qkv + N_HEADS])
    wf_hi = w_f.astype(BF16)
    wf_lo = (w_f - wf_hi.astype(F32)).astype(BF16)
    bf_pad = _pad_lanes(b_forget.reshape(1, N_HEADS))
    scale = HEAD_DIM ** -0.5
    ones = jnp.ones((HEAD_DIM,), F32)
    head_norm_w = jnp.stack([qn_a * scale, kn_a, ones, qn_b * scale, kn_b, ones, ones, ones])
    wr = _pad_lanes(w_router)
    wr_hi = wr.astype(BF16)
    wr_lo = (wr - wr_hi.astype(F32)).astype(BF16)
    br_pad = jnp.concatenate([b_router.reshape(1, N_EXPERTS),
                              jnp.full((1, LANES - N_EXPERTS), NEG_INF, F32)], axis=1)

    proj, cum = _in_proj(x2, attn_norm_w, w_all, wf_hi, wf_lo, bf_pad, head_norm_w, s)
    proj3 = proj.reshape(b, s, proj.shape[1])

    y_a = _attn_a(proj3, _attn_a_bias(rel_bias))
    cum3 = cum.reshape(b, s, LANES)
    cum_rows = jnp.transpose(cum3[:, :, :N_HEADS], (0, 2, 1)).reshape(b, N_HEADS, s // TK_B, 1, TK_B)
    y_b = _attn_b(proj3, cum3, cum_rows)

    h, hn, route, cnt = _merge(x2, y_a.reshape(n, WIDTH), y_b.reshape(n, WIDTH), proj,
                                w_branch_a.astype(BF16), w_branch_b.astype(BF16), w_out.astype(BF16),
                                ffn_norm_w, wr_hi, wr_lo, br_pad)

    cap = n + TM_MOE
    blocks_per_expert = cap // TM_MOE
    top_e = route[:, :TOP_K].astype(jnp.int32)
    rank = route[:, 2 * TOP_K:3 * TOP_K].astype(jnp.int32)
    dest_flat = (top_e * cap + rank).reshape(n * TOP_K)
    counts = cnt[0, :N_EXPERTS].astype(jnp.int32)
    nblk = (counts + TM_MOE - 1) // TM_MOE
    bend = jnp.cumsum(nblk)
    bstart = bend - nblk
    max_blocks = n * TOP_K // TM_MOE + N_EXPERTS
    blk = jnp.minimum(jnp.arange(max_blocks), bend[-1] - 1)
    blk_expert = jnp.minimum(jnp.sum(blk[:, None] >= bend[None, :], axis=1), N_EXPERTS - 1).astype(jnp.int32)
    blk_in_expert = blk - bstart[blk_expert]
    blk_row = (blk_expert * blocks_per_expert + blk_in_expert).astype(jnp.int32)
    blk_tokens = jnp.clip(counts[blk_expert] - blk_in_expert * TM_MOE, 0, TM_MOE)
    blk_chunks = ((blk_tokens + TC_MOE - 1) // TC_MOE).astype(jnp.int32)
    n_blocks = bend[-1].astype(jnp.int32).reshape(1)

    xs = _dispatch(counts, dest_flat, hn, cap)
    ys = _moe(xs, blk_row, blk_expert, blk_chunks, n_blocks, w_gate_up, b_gate_up, w_down, b_down, max_blocks)
    out = _combine(dest_flat, h, route, ys)
    return out.reshape(b, s, d)
```

```python
import functools

import jax
import jax.numpy as jnp
from jax import lax
from jax.experimental import pallas as pl
from jax.experimental.pallas import tpu as pltpu

D_MODEL = 2048
CHUNK = 64
LEFT_CHUNKS = 8
BAND = (LEFT_CHUNKS + 1) * CHUNK
HEAD_DIM = 128
N_HEADS = 8
WIDTH = N_HEADS * HEAD_DIM
REL_CLIP = 256
N_EXPERTS = 32
TOP_K = 4
D_EXPERT = D_MODEL
SWIGLU_LIMIT = 7.0
SWIGLU_ALPHA = 1.702
NORM_EPS = 1e-5
NEG_INF = -1e30

LANES = 128
SUBLANES = 8
VMEM_LIMIT = 56 * 1024 * 1024

TM_PROJ = 1024
TN_PROJ = 1024
TQ_A = 256
WIN_A = TQ_A + LEFT_CHUNKS * CHUNK
TQ_B = 512
TK_B = 512
HB_B = 4
TM_MERGE = 256
TM_ROUTE = 256
TM_MOE = 1024
TC_MOE = 256
TF_MOE = 256

F32 = jnp.float32
BF16 = jnp.bfloat16
U32 = jnp.uint32


def _dot(a, b):
    return jnp.dot(a, b, preferred_element_type=F32)


def _dot_nt(a, b):
    return lax.dot_general(a, b, (((1,), (1,)), ((), ())), preferred_element_type=F32)


def _pack_bf16_pair(lo, hi):
    lo_bits = lax.bitcast_convert_type(lo.astype(BF16).astype(F32), U32)
    hi_bits = lax.bitcast_convert_type(hi.astype(BF16).astype(F32), U32)
    return (lo_bits >> 16) | hi_bits


def _unpack_bf16_pair(w):
    lo = lax.bitcast_convert_type(w << 16, F32)
    hi = lax.bitcast_convert_type(w & jnp.uint32(0xFFFF0000), F32)
    return lo, hi


def _split3(x):
    hi = x.astype(BF16)
    r = x - hi.astype(F32)
    mid = r.astype(BF16)
    lo = (r - mid.astype(F32)).astype(BF16)
    return hi, mid, lo


def _dot_f32x(x, w_hi, w_lo):
    hi, mid, _ = _split3(x)
    return _dot(hi, w_hi) + (_dot(hi, w_lo) + _dot(mid, w_hi))


def _in_proj_kernel(x_ref, nw_ref, w_ref, wf_hi_ref, wf_lo_ref, bf_ref, hn_ref, tri_ref,
                    out_ref, cum_ref, xn_sc, carry_sc, *, tiles_per_seq):
    i = pl.program_id(0)
    j = pl.program_id(1)

    @pl.when(j == 0)
    def _():
        x = x_ref[...]
        ms = jnp.mean(x * x, axis=-1, keepdims=True)
        xn = x * lax.rsqrt(ms + NORM_EPS) * nw_ref[...]
        xn_sc[...] = xn.astype(BF16)
        z = _dot_f32x(xn, wf_hi_ref[...], wf_lo_ref[...]) + bf_ref[...]
        logf = jnp.minimum(z, 0.0) - jnp.log1p(jnp.exp(-jnp.abs(z)))
        hi, mid, lo = _split3(logf)
        tri = tri_ref[...]
        c = _dot(tri, hi) + (_dot(tri, mid) + _dot(tri, lo))
        carry = jnp.where(i % tiles_per_seq == 0, 0.0, carry_sc[...])
        c = c + carry
        cum_ref[...] = c
        carry_sc[...] = c[-1:, :]

    acc = _dot(xn_sc[...], w_ref[...])
    is_norm = jnp.logical_and(j < 6, jnp.logical_and(j != 2, j != 5))

    @pl.when(is_norm)
    def _():
        w = hn_ref[pl.ds(j, 1), :]
        for h in range(N_HEADS):
            sl = slice(h * HEAD_DIM, (h + 1) * HEAD_DIM)
            t = acc[:, sl]
            ms = jnp.mean(t * t, axis=-1, keepdims=True)
            out_ref[:, sl] = (t * lax.rsqrt(ms + NORM_EPS) * w).astype(BF16)

    @pl.when(jnp.logical_or(j == 2, j == 5))
    def _():
        out_ref[...] = acc.astype(BF16)

    @pl.when(j >= 6)
    def _():
        out_ref[...] = jax.nn.sigmoid(acc).astype(BF16)


def _in_proj(x2, attn_norm_w, w_all, wf_hi, wf_lo, bf_pad, head_norm_w, seq, tm=TM_PROJ, name="in_proj"):
    n, d = x2.shape
    n_col = w_all.shape[1] // TN_PROJ
    tri = (jnp.arange(tm)[:, None] >= jnp.arange(tm)[None, :]).astype(BF16)
    return pl.pallas_call(
        functools.partial(_in_proj_kernel, tiles_per_seq=seq // tm),
        name=name,
        grid=(n // tm, n_col),
        in_specs=[
            pl.BlockSpec((tm, d), lambda i, j: (i, 0)),
            pl.BlockSpec((1, d), lambda i, j: (0, 0)),
            pl.BlockSpec((d, TN_PROJ), lambda i, j: (0, j)),
            pl.BlockSpec((d, LANES), lambda i, j: (0, 0)),
            pl.BlockSpec((d, LANES), lambda i, j: (0, 0)),
            pl.BlockSpec((1, LANES), lambda i, j: (0, 0)),
            pl.BlockSpec(head_norm_w.shape, lambda i, j: (0, 0)),
            pl.BlockSpec((tm, tm), lambda i, j: (0, 0)),
        ],
        out_specs=[
            pl.BlockSpec((tm, TN_PROJ), lambda i, j: (i, j)),
            pl.BlockSpec((tm, LANES), lambda i, j: (i, 0)),
        ],
        out_shape=[
            jax.ShapeDtypeStruct((n, w_all.shape[1]), BF16),
            jax.ShapeDtypeStruct((n, LANES), F32),
        ],
        scratch_shapes=[pltpu.VMEM((tm, d), BF16), pltpu.VMEM((1, LANES), F32)],
        compiler_params=pltpu.CompilerParams(
            dimension_semantics=("arbitrary", "arbitrary"), vmem_limit_bytes=VMEM_LIMIT),
    )(x2, attn_norm_w.reshape(1, d), w_all, wf_hi, wf_lo, bf_pad, head_norm_w, tri)


def _attn_a_kernel(q_ref, k0_ref, k1_ref, k2_ref, v0_ref, v1_ref, v2_ref, ext_ref, o_ref, bias_ref):
    qb = pl.program_id(1)

    @pl.when(jnp.logical_and(pl.program_id(0) == 0, qb == 0))
    def _():
        ext_len = ext_ref.shape[1]
        r = lax.broadcasted_iota(jnp.int32, (TQ_A, WIN_A), 0)
        c = lax.broadcasted_iota(jnp.int32, (TQ_A, WIN_A), 1)
        lo = (r // CHUNK) * CHUNK
        in_band = jnp.logical_and(c >= lo, c < lo + BAND)
        for h in range(N_HEADS):
            rows = jnp.broadcast_to(ext_ref[pl.ds(h, 1), :], (TQ_A, ext_len))
            win = pltpu.roll(rows, ext_len - (TQ_A - 1), 1, stride=1, stride_axis=0)[:, :WIN_A]
            bias_ref[h] = jnp.where(in_band, win, NEG_INF)
    k_refs = (k0_ref, k1_ref, k2_ref)
    v_refs = (v0_ref, v1_ref, v2_ref)
    n_win = len(k_refs)
    for h in range(N_HEADS):
        sl = slice(h * HEAD_DIM, (h + 1) * HEAD_DIM)
        q = q_ref[:, sl]
        scores = []
        for t in range(n_win):
            s = _dot_nt(q, k_refs[t][:, sl]) + bias_ref[h, :, t * TQ_A:(t + 1) * TQ_A]
            if t < n_win - 1:
                s = jnp.where(qb >= n_win - 1 - t, s, NEG_INF)
            scores.append(s)
        m = functools.reduce(jnp.maximum, [jnp.max(s, axis=-1, keepdims=True) for s in scores])
        ps = [jnp.exp(s - m) for s in scores]
        l = functools.reduce(jnp.add, [jnp.sum(p, axis=-1, keepdims=True) for p in ps])
        o = functools.reduce(jnp.add, [_dot(ps[t].astype(BF16), v_refs[t][:, sl]) for t in range(n_win)])
        o_ref[:, sl] = (o / l).astype(BF16)


def _attn_a_ext_table(rel_bias):
    j = jnp.arange(TQ_A + WIN_A)
    rel = (TQ_A - 1 + LEFT_CHUNKS * CHUNK) - j
    return rel_bias[:, jnp.clip(rel, -(CHUNK - 1), REL_CLIP) + (CHUNK - 1)].astype(F32)


def _attn_a(proj3, ext):
    b, s, _ = proj3.shape
    nq = s // TQ_A
    n_win = WIN_A // TQ_A

    def kv_spec(group, t):
        return pl.BlockSpec((None, TQ_A, WIDTH),
                            lambda bi, qi: (bi, jnp.maximum(qi - (n_win - 1) + t, 0), group))

    return pl.pallas_call(
        _attn_a_kernel,
        name="attn_a",
        grid=(b, nq),
        in_specs=[pl.BlockSpec((None, TQ_A, WIDTH), lambda bi, qi: (bi, qi, 0))]
        + [kv_spec(1, t) for t in range(n_win)]
        + [kv_spec(2, t) for t in range(n_win)]
        + [pl.BlockSpec(ext.shape, lambda bi, qi: (0, 0))],
        out_specs=pl.BlockSpec((None, TQ_A, WIDTH), lambda bi, qi: (bi, qi, 0)),
        out_shape=jax.ShapeDtypeStruct((b, s, WIDTH), BF16),
        scratch_shapes=[pltpu.VMEM((N_HEADS, TQ_A, WIN_A), F32)],
        compiler_params=pltpu.CompilerParams(
            dimension_semantics=("arbitrary", "arbitrary"), vmem_limit_bytes=VMEM_LIMIT),
    )(proj3, *([proj3] * (2 * n_win)), ext)


def _attn_b_kernel(q_ref, k_ref, v_ref, cq_ref, ck_ref, o_ref, m_sc, l_sc, acc_sc, *, hb):
    hp = pl.program_id(1)
    qi = pl.program_id(2)
    n_chunk = TK_B // LANES
    lane = lax.broadcasted_iota(jnp.int32, (TQ_B, LANES), 1)
    row = lax.broadcasted_iota(jnp.int32, (TQ_B, LANES), 0)
    qs, cqs = [], []
    for a in range(hb):
        qs.append(q_ref[:, a * HEAD_DIM:(a + 1) * HEAD_DIM])
        cq = jnp.sum(jnp.where(lane == hp * hb + a, cq_ref[...], 0.0), axis=-1, keepdims=True)
        cqs.append(jnp.broadcast_to(cq, (TQ_B, LANES)))
    m_sc[...] = jnp.full(m_sc.shape, NEG_INF, F32)
    l_sc[...] = jnp.zeros(l_sc.shape, F32)
    acc_sc[...] = jnp.zeros(acc_sc.shape, F32)

    def block(kb, masked):
        for a in range(hb):
            sl = slice(a * HEAD_DIM, (a + 1) * HEAD_DIM)
            k = k_ref[pl.ds(kb * TK_B, TK_B), sl]
            v = v_ref[pl.ds(kb * TK_B, TK_B), sl]
            s = _dot_nt(qs[a], k)
            ck = ck_ref[a, kb]
            chunks = []
            for c in range(n_chunk):
                cs = slice(c * LANES, (c + 1) * LANES)
                sc = s[:, cs] + (cqs[a] - ck[:, cs])
                if masked:
                    sc = jnp.where(lane + c * LANES <= row, sc, NEG_INF)
                chunks.append(sc)
            m_old = m_sc[a]
            m_new = jnp.maximum(m_old, jnp.max(functools.reduce(jnp.maximum, chunks), axis=-1, keepdims=True))
            alpha = jnp.exp(m_old - m_new)
            ps = [jnp.exp(sc - m_new) for sc in chunks]
            l_sc[a] = alpha * l_sc[a] + jnp.sum(functools.reduce(jnp.add, ps), axis=-1, keepdims=True)
            p = jnp.concatenate([pc.astype(BF16) for pc in ps], axis=1)
            acc_sc[a] = alpha * acc_sc[a] + _dot(p, v)
            m_sc[a] = m_new

    def body(kb, carry):
        block(kb, False)
        return carry

    lax.fori_loop(0, qi, body, 0)
    block(qi, True)
    for a in range(hb):
        o_ref[:, a * HEAD_DIM:(a + 1) * HEAD_DIM] = (acc_sc[a] / l_sc[a]).astype(BF16)


def _attn_b(proj3, cum3, cum_rows, hb=HB_B, name="attn_b"):
    b, s, _ = proj3.shape
    nq = s // TQ_B
    wb = hb * HEAD_DIM
    q0 = 3 * WIDTH // wb
    gstep = WIDTH // wb
    return pl.pallas_call(
        functools.partial(_attn_b_kernel, hb=hb),
        name=name,
        grid=(b, N_HEADS // hb, nq),
        in_specs=[
            pl.BlockSpec((None, TQ_B, wb), lambda bi, hp, qi: (bi, qi, q0 + hp)),
            pl.BlockSpec((None, s, wb), lambda bi, hp, qi: (bi, 0, q0 + gstep + hp)),
            pl.BlockSpec((None, s, wb), lambda bi, hp, qi: (bi, 0, q0 + 2 * gstep + hp)),
            pl.BlockSpec((None, TQ_B, LANES), lambda bi, hp, qi: (bi, qi, 0)),
            pl.BlockSpec((None, hb, s // TK_B, 1, TK_B), lambda bi, hp, qi: (bi, hp, 0, 0, 0)),
        ],
        out_specs=pl.BlockSpec((None, TQ_B, wb), lambda bi, hp, qi: (bi, qi, hp)),
        out_shape=jax.ShapeDtypeStruct((b, s, WIDTH), BF16),
        scratch_shapes=[pltpu.VMEM((hb, TQ_B, LANES), F32), pltpu.VMEM((hb, TQ_B, LANES), F32),
                        pltpu.VMEM((hb, TQ_B, HEAD_DIM), F32)],
        compiler_params=pltpu.CompilerParams(
            dimension_semantics=("arbitrary", "arbitrary", "arbitrary"), vmem_limit_bytes=VMEM_LIMIT),
    )(proj3, proj3, proj3, cum3, cum_rows)


def _merge_kernel(x_ref, ya_ref, yb_ref, ga_ref, gb_ref, wa_ref, wb_ref, wo_ref, fw_ref,
                  wr_hi_ref, wr_lo_ref, br_ref, tri_ref,
                  h_ref, hn_ref, route_ref, cnt_ref, carry_sc):
    i = pl.program_id(0)

    @pl.when(i == 0)
    def _():
        carry_sc[...] = jnp.zeros_like(carry_sc)

    za = _dot(ya_ref[...], wa_ref[...])
    zb = _dot(yb_ref[...], wb_ref[...])
    z = ga_ref[...].astype(F32) * za + gb_ref[...].astype(F32) * zb
    h = x_ref[...] + _dot(z.astype(BF16), wo_ref[...])
    h_ref[...] = h
    ms = jnp.mean(h * h, axis=-1, keepdims=True)
    hn = h * lax.rsqrt(ms + NORM_EPS) * fw_ref[...]
    half = hn.shape[1] // 2
    hn_ref[...] = _pack_bf16_pair(hn[:, :half], hn[:, half:])

    logits = _dot_f32x(hn, wr_hi_ref[...], wr_lo_ref[...]) + br_ref[...]
    tm = logits.shape[0]
    lane = lax.broadcasted_iota(jnp.int32, (tm, LANES), 1).astype(F32)
    work = logits
    vals, idxs = [], []
    for _ in range(TOP_K):
        m = jnp.max(work, axis=-1, keepdims=True)
        ix = jnp.min(jnp.where(work == m, lane, float(LANES)), axis=-1, keepdims=True)
        vals.append(m)
        idxs.append(ix)
        work = jnp.where(lane == ix, -jnp.inf, work)
    es = [jnp.exp(v - vals[0]) for v in vals]
    denom = functools.reduce(jnp.add, es)
    onehots = [(lane == ix).astype(F32) for ix in idxs]
    cnt = functools.reduce(jnp.add, onehots)
    before = _dot(tri_ref[...], cnt.astype(BF16)) + carry_sc[...]
    route = jnp.zeros((tm, LANES), F32)
    for k in range(TOP_K):
        rank = jnp.sum(onehots[k] * before, axis=-1, keepdims=True)
        route = jnp.where(lane == float(k), idxs[k], route)
        route = jnp.where(lane == float(TOP_K + k), es[k] / denom, route)
        route = jnp.where(lane == float(2 * TOP_K + k), rank, route)
    route_ref[...] = route
    total = carry_sc[...] + jnp.sum(cnt, axis=0, keepdims=True)
    carry_sc[...] = total
    cnt_ref[...] = total


def _merge(x2, ya, yb, proj, w_a, w_b, w_o, ffn_norm_w, wr_hi, wr_lo, br_pad):
    n, d = x2.shape
    tm = TM_MERGE
    tri = (jnp.arange(tm)[:, None] > jnp.arange(tm)[None, :]).astype(BF16)
    ga_blk = 3 * WIDTH * 2 // d
    const = lambda i: (0, 0)
    return pl.pallas_call(
        _merge_kernel,
        name="merge",
        grid=(n // tm,),
        in_specs=[
            pl.BlockSpec((tm, d), lambda i: (i, 0)),
            pl.BlockSpec((tm, WIDTH), lambda i: (i, 0)),
            pl.BlockSpec((tm, WIDTH), lambda i: (i, 0)),
            pl.BlockSpec((tm, d), lambda i: (i, ga_blk)),
            pl.BlockSpec((tm, d), lambda i: (i, ga_blk + 1)),
            pl.BlockSpec((WIDTH, d), const, pipeline_mode=pl.Buffered(1)),
            pl.BlockSpec((WIDTH, d), const, pipeline_mode=pl.Buffered(1)),
            pl.BlockSpec((d, d), const, pipeline_mode=pl.Buffered(1)),
            pl.BlockSpec((1, d), const),
            pl.BlockSpec((d, LANES), const),
            pl.BlockSpec((d, LANES), const),
            pl.BlockSpec((1, LANES), const),
            pl.BlockSpec((tm, tm), const),
        ],
        out_specs=[
            pl.BlockSpec((tm, d), lambda i: (i, 0)),
            pl.BlockSpec((tm, d // 2), lambda i: (i, 0)),
            pl.BlockSpec((tm, LANES), lambda i: (i, 0)),
            pl.BlockSpec((1, LANES), const),
        ],
        out_shape=[
            jax.ShapeDtypeStruct((n, d), F32),
            jax.ShapeDtypeStruct((n, d // 2), U32),
            jax.ShapeDtypeStruct((n, LANES), F32),
            jax.ShapeDtypeStruct((1, LANES), F32),
        ],
        scratch_shapes=[pltpu.VMEM((1, LANES), F32)],
        compiler_params=pltpu.CompilerParams(
            dimension_semantics=("arbitrary",), vmem_limit_bytes=VMEM_LIMIT),
    )(x2, ya, yb, proj, proj, w_a, w_b, w_o, ffn_norm_w.reshape(1, d), wr_hi, wr_lo, br_pad, tri)


def _dispatch_kernel(cnt_ref, dest_ref, hn_ref, xs_ref, zero_sc, sem, zsem, *, cap):
    i = pl.program_id(0)

    def row_copy(t, k):
        return pltpu.make_async_copy(hn_ref.at[pl.ds(t, 1)], xs_ref.at[pl.ds(dest_ref[t * TOP_K + k], 1)], sem)

    def issue(t, carry):
        for k in range(TOP_K):
            row_copy(t, k).start(priority=k % 2)
        return carry

    lax.fori_loop(0, TM_ROUTE, issue, 0, unroll=8)

    @pl.when(i == pl.num_programs(0) - 1)
    def _():
        zero_sc[...] = jnp.zeros(zero_sc.shape, zero_sc.dtype)

        def zero_fill(e, start):
            first = e * cap + cnt_ref[e]
            aligned = pl.multiple_of((first + SUBLANES - 1) // SUBLANES * SUBLANES, SUBLANES)
            for r in range(SUBLANES - 1):
                row = pltpu.make_async_copy(zero_sc.at[pl.ds(0, 1)], xs_ref.at[pl.ds(first + r, 1)], zsem)
                pl.when(first + r < aligned)(row.start if start else row.wait)
            block = pltpu.make_async_copy(zero_sc, xs_ref.at[pl.ds(aligned, TC_MOE)], zsem)
            block.start() if start else block.wait()

        for e in range(N_EXPERTS):
            zero_fill(e, True)
        for e in range(N_EXPERTS):
            zero_fill(e, False)

    def drain(t, carry):
        for k in range(TOP_K):
            row_copy(t, k).wait()
        return carry

    lax.fori_loop(0, TM_ROUTE, drain, 0, unroll=8)


def _dispatch(counts, dest_flat, hn, cap):
    n, d = hn.shape
    grid_spec = pltpu.PrefetchScalarGridSpec(
        num_scalar_prefetch=1,
        grid=(n // TM_ROUTE,),
        in_specs=[
            pl.BlockSpec((TM_ROUTE * TOP_K,), lambda i, cnt: (i,), memory_space=pltpu.SMEM),
            pl.BlockSpec((TM_ROUTE, d), lambda i, cnt: (i, 0)),
        ],
        out_specs=pl.BlockSpec(memory_space=pl.ANY),
        scratch_shapes=[pltpu.VMEM((TC_MOE, d), hn.dtype), pltpu.SemaphoreType.DMA, pltpu.SemaphoreType.DMA],
    )
    return pl.pallas_call(
        functools.partial(_dispatch_kernel, cap=cap),
        name="dispatch",
        grid_spec=grid_spec,
        out_shape=jax.ShapeDtypeStruct((N_EXPERTS * cap, d), hn.dtype),
        compiler_params=pltpu.CompilerParams(
            dimension_semantics=("arbitrary",), vmem_limit_bytes=VMEM_LIMIT),
    )(counts, dest_flat, hn)


def _moe_kernel(br_ref, be_ref, nc_ref, nb_ref, x_ref, wg_ref, wu_ref, wd_ref, bg_ref, bu_ref, bd_ref, o_ref,
                x_sc, acc_sc):
    i = pl.program_id(0)
    j = pl.program_id(1)
    d = acc_sc.shape[1]
    half = d // 2

    def process(off, m):
        rows = pl.ds(off, m)

        @pl.when(j == 0)
        def _():
            lo, hi = _unpack_bf16_pair(x_ref[rows, :])
            x_sc[rows, :half] = lo.astype(BF16)
            x_sc[rows, half:] = hi.astype(BF16)
            acc_sc[rows, :] = jnp.broadcast_to(bd_ref[...], (m, d))

        x = x_sc[rows, :]
        g = _dot(x, wg_ref[...].astype(BF16)) + bg_ref[...]
        u = _dot(x, wu_ref[...].astype(BF16)) + bu_ref[...]
        g = jnp.minimum(g, SWIGLU_LIMIT)
        u = jnp.clip(u, -SWIGLU_LIMIT, SWIGLU_LIMIT)
        hb = (u + 1.0) * (g * jax.nn.sigmoid(SWIGLU_ALPHA * g))
        acc_sc[rows, :] += _dot(hb.astype(BF16), wd_ref[...].astype(BF16))

        @pl.when(j == pl.num_programs(1) - 1)
        def _():
            y = acc_sc[rows, :]
            o_ref[rows, :] = _pack_bf16_pair(y[:, :half], y[:, half:])

    @pl.when(i < nb_ref[0])
    def _():
        n_chunks = nc_ref[i]
        pl.when(n_chunks == 4)(lambda: process(0, 4 * TC_MOE))
        pl.when(jnp.logical_or(n_chunks == 2, n_chunks == 3))(lambda: process(0, 2 * TC_MOE))
        pl.when(jnp.logical_or(n_chunks == 1, n_chunks == 3))(
            lambda: process(pl.multiple_of((n_chunks - 1) * TC_MOE, TC_MOE), TC_MOE))


def _moe(xs, blk_row, blk_expert, blk_chunks, n_blocks, w_gate_up, b_gate_up, w_down, b_down, max_blocks):
    rows, half = xs.shape
    d = 2 * half
    assert TM_MOE == 4 * TC_MOE
    nj = D_EXPERT // TF_MOE
    b_gu = b_gate_up.reshape(N_EXPERTS, 1, 2 * D_EXPERT)
    b_dn = b_down.reshape(N_EXPERTS, 1, d)

    def jj(i, j, nb):
        return jnp.where(i < nb[0], j, nj - 1)

    grid_spec = pltpu.PrefetchScalarGridSpec(
        num_scalar_prefetch=4,
        grid=(max_blocks, nj),
        in_specs=[
            pl.BlockSpec((TM_MOE, half), lambda i, j, br, be, nc, nb: (br[i], 0)),
            pl.BlockSpec((None, d, TF_MOE), lambda i, j, br, be, nc, nb: (be[i], 0, jj(i, j, nb))),
            pl.BlockSpec((None, d, TF_MOE), lambda i, j, br, be, nc, nb: (be[i], 0, jj(i, j, nb) + nj)),
            pl.BlockSpec((None, TF_MOE, d), lambda i, j, br, be, nc, nb: (be[i], jj(i, j, nb), 0)),
            pl.BlockSpec((None, 1, TF_MOE), lambda i, j, br, be, nc, nb: (be[i], 0, jj(i, j, nb))),
            pl.BlockSpec((None, 1, TF_MOE), lambda i, j, br, be, nc, nb: (be[i], 0, jj(i, j, nb) + nj)),
            pl.BlockSpec((None, 1, d), lambda i, j, br, be, nc, nb: (be[i], 0, 0)),
        ],
        out_specs=pl.BlockSpec((TM_MOE, half), lambda i, j, br, be, nc, nb: (br[i], 0)),
        scratch_shapes=[pltpu.VMEM((TM_MOE, d), BF16), pltpu.VMEM((TM_MOE, d), F32)],
    )
    return pl.pallas_call(
        _moe_kernel,
        name="moe",
        grid_spec=grid_spec,
        out_shape=jax.ShapeDtypeStruct((rows, half), U32),
        compiler_params=pltpu.CompilerParams(
            dimension_semantics=("arbitrary", "arbitrary"), vmem_limit_bytes=VMEM_LIMIT),
    )(blk_row, blk_expert, blk_chunks, n_blocks, xs, w_gate_up, w_gate_up, w_down, b_gu, b_gu, b_dn)


def _combine_kernel(dest_ref, h_ref, route_ref, ys_ref, o_ref, buf, sem):
    def row_copy(t, k):
        return pltpu.make_async_copy(ys_ref.at[pl.ds(dest_ref[t * TOP_K + k], 1)], buf.at[k, pl.ds(t, 1)], sem)

    def issue(t, carry):
        for k in range(TOP_K):
            row_copy(t, k).start(priority=k % 2)
        return carry

    def drain(t, carry):
        for k in range(TOP_K):
            row_copy(t, k).wait()
        return carry

    lax.fori_loop(0, TM_ROUTE, issue, 0, unroll=8)
    lax.fori_loop(0, TM_ROUTE, drain, 0, unroll=8)
    route = route_ref[...]
    half = buf.shape[2]
    acc_lo = h_ref[:, :half]
    acc_hi = h_ref[:, half:]
    for k in range(TOP_K):
        gate = route[:, TOP_K + k:TOP_K + k + 1]
        lo, hi = _unpack_bf16_pair(buf[k])
        acc_lo = acc_lo + gate * lo
        acc_hi = acc_hi + gate * hi
    o_ref[:, :half] = acc_lo
    o_ref[:, half:] = acc_hi


def _combine(dest_flat, h, route, ys):
    n, d = h.shape
    return pl.pallas_call(
        _combine_kernel,
        name="combine",
        grid=(n // TM_ROUTE,),
        in_specs=[
            pl.BlockSpec((TM_ROUTE * TOP_K,), lambda i: (i,), memory_space=pltpu.SMEM),
            pl.BlockSpec((TM_ROUTE, d), lambda i: (i, 0)),
            pl.BlockSpec((TM_ROUTE, LANES), lambda i: (i, 0)),
            pl.BlockSpec(memory_space=pl.ANY),
        ],
        out_specs=pl.BlockSpec((TM_ROUTE, d), lambda i: (i, 0)),
        out_shape=jax.ShapeDtypeStruct((n, d), F32),
        scratch_shapes=[pltpu.VMEM((TOP_K, TM_ROUTE, d // 2), U32), pltpu.SemaphoreType.DMA],
        compiler_params=pltpu.CompilerParams(
            dimension_semantics=("arbitrary",), vmem_limit_bytes=VMEM_LIMIT),
    )(dest_flat, h, route, ys)


def _pad_lanes(a):
    return jnp.pad(a, ((0, 0), (0, LANES - a.shape[1])))


def kernel(x, attn_norm_w, w_in, b_forget, qn_a, kn_a, qn_b, kn_b, rel_bias, w_branch_a, w_branch_b,
           w_out, ffn_norm_w, w_router, b_router, w_gate_up, b_gate_up, w_down, b_down):
    b, s, d = x.shape
    n = b * s
    x2 = x.reshape(n, d)

    n_qkv = 6 * WIDTH
    w_all = jnp.concatenate([w_in[:, :n_qkv], w_in[:, n_qkv + N_HEADS:]], axis=1).astype(BF16)
    w_f = _pad_lanes(w_in[:, n_qkv:n_qkv + N_HEADS])
    wf_hi = w_f.astype(BF16)
    wf_lo = (w_f - wf_hi.astype(F32)).astype(BF16)
    bf_pad = _pad_lanes(b_forget.reshape(1, N_HEADS))
    scale = HEAD_DIM ** -0.5
    ones = jnp.ones((HEAD_DIM,), F32)
    head_norm_w = jnp.stack([qn_a * scale, kn_a, ones, qn_b * scale, kn_b, ones, ones, ones])
    wr = _pad_lanes(w_router)
    wr_hi = wr.astype(BF16)
    wr_lo = (wr - wr_hi.astype(F32)).astype(BF16)
    br_pad = jnp.concatenate([b_router.reshape(1, N_EXPERTS),
                              jnp.full((1, LANES - N_EXPERTS), NEG_INF, F32)], axis=1)

    proj, cum = _in_proj(x2, attn_norm_w, w_all, wf_hi, wf_lo, bf_pad, head_norm_w, s)
    proj3 = proj.reshape(b, s, proj.shape[1])

    y_a = _attn_a(proj3, _attn_a_ext_table(rel_bias))
    cum3 = cum.reshape(b, s, LANES)
    cum_rows = jnp.transpose(cum3[:, :, :N_HEADS], (0, 2, 1)).reshape(b, N_HEADS, s // TK_B, 1, TK_B)
    y_b = _attn_b(proj3, cum3, cum_rows)

    h, hn, route, cnt = _merge(x2, y_a.reshape(n, WIDTH), y_b.reshape(n, WIDTH), proj,
                                w_branch_a.astype(BF16), w_branch_b.astype(BF16), w_out.astype(BF16),
                                ffn_norm_w, wr_hi, wr_lo, br_pad)

    cap = n + TM_MOE
    blocks_per_expert = cap // TM_MOE
    top_e = route[:, :TOP_K].astype(jnp.int32)
    rank = route[:, 2 * TOP_K:3 * TOP_K].astype(jnp.int32)
    dest_flat = (top_e * cap + rank).reshape(n * TOP_K)
    counts = cnt[0, :N_EXPERTS].astype(jnp.int32)
    nblk = (counts + TM_MOE - 1) // TM_MOE
    bend = jnp.cumsum(nblk)
    bstart = bend - nblk
    max_blocks = n * TOP_K // TM_MOE + N_EXPERTS
    blk = jnp.minimum(jnp.arange(max_blocks), bend[-1] - 1)
    blk_expert = jnp.minimum(jnp.sum(blk[:, None] >= bend[None, :], axis=1), N_EXPERTS - 1).astype(jnp.int32)
    blk_in_expert = blk - bstart[blk_expert]
    blk_row = (blk_expert * blocks_per_expert + blk_in_expert).astype(jnp.int32)
    blk_tokens = jnp.clip(counts[blk_expert] - blk_in_expert * TM_MOE, 0, TM_MOE)
    blk_chunks = ((blk_tokens + TC_MOE - 1) // TC_MOE).astype(jnp.int32)
    n_blocks = bend[-1].astype(jnp.int32).reshape(1)

    xs = _dispatch(counts, dest_flat, hn, cap)
    ys = _moe(xs, blk_row, blk_expert, blk_chunks, n_blocks, w_gate_up, b_gate_up, w_down, b_down, max_blocks)
    out = _combine(dest_flat, h, route, ys)
    return out.reshape(b, s, d)
```

```python
import functools

import jax
import jax.numpy as jnp
from jax import lax
from jax.experimental import pallas as pl
from jax.experimental.pallas import tpu as pltpu

D_MODEL = 2048
CHUNK = 64
LEFT_CHUNKS = 8
BAND = (LEFT_CHUNKS + 1) * CHUNK
HEAD_DIM = 128
N_HEADS = 8
WIDTH = N_HEADS * HEAD_DIM
REL_CLIP = 256
N_EXPERTS = 32
TOP_K = 4
D_EXPERT = D_MODEL
SWIGLU_LIMIT = 7.0
SWIGLU_ALPHA = 1.702
NORM_EPS = 1e-5
NEG_INF = -1e30

LANES = 128
SUBLANES = 8
VMEM_LIMIT = 56 * 1024 * 1024
VMEM_LIMIT_MOE = 60 * 1024 * 1024

TM_PROJ = 1024
TN_PROJ = 1024
TQ_A = 256
WIN_A = TQ_A + LEFT_CHUNKS * CHUNK
TQ_B = 512
TK_B = 512
HB_B = 4
TM_MERGE = 256
TM_ROUTE = 256
TM_MOE = 1024
TC_MOE = 256
TF_MOE = 512

F32 = jnp.float32
BF16 = jnp.bfloat16
U32 = jnp.uint32


def _dot(a, b):
    return jnp.dot(a, b, preferred_element_type=F32)


def _dot_nt(a, b):
    return lax.dot_general(a, b, (((1,), (1,)), ((), ())), preferred_element_type=F32)


def _pack_bf16_pair(lo, hi):
    lo_bits = lax.bitcast_convert_type(lo.astype(BF16).astype(F32), U32)
    hi_bits = lax.bitcast_convert_type(hi.astype(BF16).astype(F32), U32)
    return (lo_bits >> 16) | hi_bits


def _unpack_bf16_pair(w):
    lo = lax.bitcast_convert_type(w << 16, F32)
    hi = lax.bitcast_convert_type(w & jnp.uint32(0xFFFF0000), F32)
    return lo, hi


def _split3(x):
    hi = x.astype(BF16)
    r = x - hi.astype(F32)
    mid = r.astype(BF16)
    lo = (r - mid.astype(F32)).astype(BF16)
    return hi, mid, lo


def _dot_f32x(x, w_hi, w_lo):
    hi, mid, _ = _split3(x)
    return _dot(hi, w_hi) + (_dot(hi, w_lo) + _dot(mid, w_hi))


def _in_proj_kernel(x_ref, nw_ref, w_ref, wf_ref, bf_ref, hn_ref, tri_ref,
                    out_ref, cum_ref, xn_sc, carry_sc, *, tiles_per_seq):
    i = pl.program_id(0)
    j = pl.program_id(1)

    @pl.when(j == 0)
    def _():
        x = x_ref[...]
        ms = jnp.mean(x * x, axis=-1, keepdims=True)
        xn = x * lax.rsqrt(ms + NORM_EPS) * nw_ref[...]
        xn_sc[...] = xn.astype(BF16)
        lane = lax.broadcasted_iota(jnp.int32, wf_ref.shape, 1)
        wf = jnp.where(lane < N_HEADS, wf_ref[...], 0.0)
        wf_hi = wf.astype(BF16)
        wf_lo = (wf - wf_hi.astype(F32)).astype(BF16)
        z = _dot_f32x(xn, wf_hi, wf_lo) + bf_ref[...]
        logf = jnp.minimum(z, 0.0) - jnp.log1p(jnp.exp(-jnp.abs(z)))
        hi, mid, lo = _split3(logf)
        tri = tri_ref[...]
        c = _dot(tri, hi) + (_dot(tri, mid) + _dot(tri, lo))
        carry = jnp.where(i % tiles_per_seq == 0, 0.0, carry_sc[...])
        c = c + carry
        cum_ref[...] = c
        carry_sc[...] = c[-1:, :]

    acc = _dot(xn_sc[...], w_ref[...])
    is_norm = jnp.logical_and(j < 6, jnp.logical_and(j != 2, j != 5))

    @pl.when(is_norm)
    def _():
        w = hn_ref[pl.ds(j, 1), :]
        for h in range(N_HEADS):
            sl = slice(h * HEAD_DIM, (h + 1) * HEAD_DIM)
            t = acc[:, sl]
            ms = jnp.mean(t * t, axis=-1, keepdims=True)
            out_ref[:, sl] = (t * lax.rsqrt(ms + NORM_EPS) * w).astype(BF16)

    @pl.when(jnp.logical_or(j == 2, j == 5))
    def _():
        out_ref[...] = acc.astype(BF16)

    @pl.when(j >= 6)
    def _():
        out_ref[...] = jax.nn.sigmoid(acc).astype(BF16)


def _in_proj(x2, attn_norm_w, w_all, w_in, bf_pad, head_norm_w, seq, tm=TM_PROJ, name="in_proj"):
    n, d = x2.shape
    n_col = w_all.shape[1] // TN_PROJ
    tri = (jnp.arange(tm)[:, None] >= jnp.arange(tm)[None, :]).astype(BF16)
    return pl.pallas_call(
        functools.partial(_in_proj_kernel, tiles_per_seq=seq // tm),
        name=name,
        grid=(n // tm, n_col),
        in_specs=[
            pl.BlockSpec((tm, d), lambda i, j: (i, 0)),
            pl.BlockSpec((1, d), lambda i, j: (0, 0)),
            pl.BlockSpec((d, TN_PROJ), lambda i, j: (0, j)),
            pl.BlockSpec((d, LANES), lambda i, j: (0, 6 * WIDTH // LANES)),
            pl.BlockSpec((1, LANES), lambda i, j: (0, 0)),
            pl.BlockSpec(head_norm_w.shape, lambda i, j: (0, 0)),
            pl.BlockSpec((tm, tm), lambda i, j: (0, 0)),
        ],
        out_specs=[
            pl.BlockSpec((tm, TN_PROJ), lambda i, j: (i, j)),
            pl.BlockSpec((tm, LANES), lambda i, j: (i, 0)),
        ],
        out_shape=[
            jax.ShapeDtypeStruct((n, w_all.shape[1]), BF16),
            jax.ShapeDtypeStruct((n, LANES), F32),
        ],
        scratch_shapes=[pltpu.VMEM((tm, d), BF16), pltpu.VMEM((1, LANES), F32)],
        compiler_params=pltpu.CompilerParams(
            dimension_semantics=("arbitrary", "arbitrary"), vmem_limit_bytes=VMEM_LIMIT),
    )(x2, attn_norm_w.reshape(1, d), w_all, w_in, bf_pad, head_norm_w, tri)


def _attn_a_kernel(q_ref, k0_ref, k1_ref, k2_ref, v0_ref, v1_ref, v2_ref, ext_ref, o_ref, bias_ref):
    qb = pl.program_id(1)

    @pl.when(jnp.logical_and(pl.program_id(0) == 0, qb == 0))
    def _():
        ext_len = ext_ref.shape[1]
        r = lax.broadcasted_iota(jnp.int32, (TQ_A, WIN_A), 0)
        c = lax.broadcasted_iota(jnp.int32, (TQ_A, WIN_A), 1)
        lo = (r // CHUNK) * CHUNK
        in_band = jnp.logical_and(c >= lo, c < lo + BAND)
        for h in range(N_HEADS):
            rows = jnp.broadcast_to(ext_ref[pl.ds(h, 1), :], (TQ_A, ext_len))
            win = pltpu.roll(rows, ext_len - (TQ_A - 1), 1, stride=1, stride_axis=0)[:, :WIN_A]
            bias_ref[h] = jnp.where(in_band, win, NEG_INF)
    k_refs = (k0_ref, k1_ref, k2_ref)
    v_refs = (v0_ref, v1_ref, v2_ref)
    n_win = len(k_refs)
    for h in range(N_HEADS):
        sl = slice(h * HEAD_DIM, (h + 1) * HEAD_DIM)
        q = q_ref[:, sl]
        scores = []
        for t in range(n_win):
            s = _dot_nt(q, k_refs[t][:, sl]) + bias_ref[h, :, t * TQ_A:(t + 1) * TQ_A]
            if t < n_win - 1:
                s = jnp.where(qb >= n_win - 1 - t, s, NEG_INF)
            scores.append(s)
        m = functools.reduce(jnp.maximum, [jnp.max(s, axis=-1, keepdims=True) for s in scores])
        ps = [jnp.exp(s - m) for s in scores]
        l = functools.reduce(jnp.add, [jnp.sum(p, axis=-1, keepdims=True) for p in ps])
        o = functools.reduce(jnp.add, [_dot(ps[t].astype(BF16), v_refs[t][:, sl]) for t in range(n_win)])
        o_ref[:, sl] = (o / l).astype(BF16)


def _attn_a_ext_table(rel_bias):
    j = jnp.arange(TQ_A + WIN_A)
    rel = (TQ_A - 1 + LEFT_CHUNKS * CHUNK) - j
    return rel_bias[:, jnp.clip(rel, -(CHUNK - 1), REL_CLIP) + (CHUNK - 1)].astype(F32)


def _attn_a(proj3, ext):
    b, s, _ = proj3.shape
    nq = s // TQ_A
    n_win = WIN_A // TQ_A

    def kv_spec(group, t):
        return pl.BlockSpec((None, TQ_A, WIDTH),
                            lambda bi, qi: (bi, jnp.maximum(qi - (n_win - 1) + t, 0), group))

    return pl.pallas_call(
        _attn_a_kernel,
        name="attn_a",
        grid=(b, nq),
        in_specs=[pl.BlockSpec((None, TQ_A, WIDTH), lambda bi, qi: (bi, qi, 0))]
        + [kv_spec(1, t) for t in range(n_win)]
        + [kv_spec(2, t) for t in range(n_win)]
        + [pl.BlockSpec(ext.shape, lambda bi, qi: (0, 0))],
        out_specs=pl.BlockSpec((None, TQ_A, WIDTH), lambda bi, qi: (bi, qi, 0)),
        out_shape=jax.ShapeDtypeStruct((b, s, WIDTH), BF16),
        scratch_shapes=[pltpu.VMEM((N_HEADS, TQ_A, WIN_A), F32)],
        compiler_params=pltpu.CompilerParams(
            dimension_semantics=("arbitrary", "arbitrary"), vmem_limit_bytes=VMEM_LIMIT),
    )(proj3, *([proj3] * (2 * n_win)), ext)


def _attn_b_kernel(q_ref, k_ref, v_ref, cq_ref, ck_ref, o_ref, m_sc, l_sc, acc_sc, *, hb):
    hp = pl.program_id(1)
    qi = pl.program_id(2)
    n_chunk = TK_B // LANES
    lane = lax.broadcasted_iota(jnp.int32, (TQ_B, LANES), 1)
    row = lax.broadcasted_iota(jnp.int32, (TQ_B, LANES), 0)
    qs, cqs = [], []
    for a in range(hb):
        qs.append(q_ref[:, a * HEAD_DIM:(a + 1) * HEAD_DIM])
        cq = jnp.sum(jnp.where(lane == hp * hb + a, cq_ref[...], 0.0), axis=-1, keepdims=True)
        cqs.append(jnp.broadcast_to(cq, (TQ_B, LANES)))
    m_sc[...] = jnp.full(m_sc.shape, NEG_INF, F32)
    l_sc[...] = jnp.zeros(l_sc.shape, F32)
    acc_sc[...] = jnp.zeros(acc_sc.shape, F32)

    def block(kb, masked):
        for a in range(hb):
            sl = slice(a * HEAD_DIM, (a + 1) * HEAD_DIM)
            k = k_ref[pl.ds(kb * TK_B, TK_B), sl]
            v = v_ref[pl.ds(kb * TK_B, TK_B), sl]
            s = _dot_nt(qs[a], k)
            ck = ck_ref[a, kb]
            chunks = []
            for c in range(n_chunk):
                cs = slice(c * LANES, (c + 1) * LANES)
                sc = s[:, cs] + (cqs[a] - ck[:, cs])
                if masked:
                    sc = jnp.where(lane + c * LANES <= row, sc, NEG_INF)
                chunks.append(sc)
            m_old = m_sc[a]
            m_new = jnp.maximum(m_old, jnp.max(functools.reduce(jnp.maximum, chunks), axis=-1, keepdims=True))
            alpha = jnp.exp(m_old - m_new)
            ps = [jnp.exp(sc - m_new) for sc in chunks]
            l_sc[a] = alpha * l_sc[a] + jnp.sum(functools.reduce(jnp.add, ps), axis=-1, keepdims=True)
            p = jnp.concatenate([pc.astype(BF16) for pc in ps], axis=1)
            acc_sc[a] = alpha * acc_sc[a] + _dot(p, v)
            m_sc[a] = m_new

    def body(kb, carry):
        block(kb, False)
        return carry

    lax.fori_loop(0, qi, body, 0)
    block(qi, True)
    for a in range(hb):
        o_ref[:, a * HEAD_DIM:(a + 1) * HEAD_DIM] = (acc_sc[a] / l_sc[a]).astype(BF16)


def _attn_b(proj3, cum3, cum_rows, hb=HB_B, name="attn_b"):
    b, s, _ = proj3.shape
    nq = s // TQ_B
    wb = hb * HEAD_DIM
    q0 = 3 * WIDTH // wb
    gstep = WIDTH // wb
    return pl.pallas_call(
        functools.partial(_attn_b_kernel, hb=hb),
        name=name,
        grid=(b, N_HEADS // hb, nq),
        in_specs=[
            pl.BlockSpec((None, TQ_B, wb), lambda bi, hp, qi: (bi, qi, q0 + hp)),
            pl.BlockSpec((None, s, wb), lambda bi, hp, qi: (bi, 0, q0 + gstep + hp)),
            pl.BlockSpec((None, s, wb), lambda bi, hp, qi: (bi, 0, q0 + 2 * gstep + hp)),
            pl.BlockSpec((None, TQ_B, LANES), lambda bi, hp, qi: (bi, qi, 0)),
            pl.BlockSpec((None, hb, s // TK_B, 1, TK_B), lambda bi, hp, qi: (bi, hp, 0, 0, 0)),
        ],
        out_specs=pl.BlockSpec((None, TQ_B, wb), lambda bi, hp, qi: (bi, qi, hp)),
        out_shape=jax.ShapeDtypeStruct((b, s, WIDTH), BF16),
        scratch_shapes=[pltpu.VMEM((hb, TQ_B, LANES), F32), pltpu.VMEM((hb, TQ_B, LANES), F32),
                        pltpu.VMEM((hb, TQ_B, HEAD_DIM), F32)],
        compiler_params=pltpu.CompilerParams(
            dimension_semantics=("arbitrary", "arbitrary", "arbitrary"), vmem_limit_bytes=VMEM_LIMIT),
    )(proj3, proj3, proj3, cum3, cum_rows)


def _merge_kernel(x_ref, ya_ref, yb_ref, ga_ref, gb_ref, wa_ref, wb_ref, wo_ref, fw_ref,
                  wr_hi_ref, wr_lo_ref, br_ref, tri_ref,
                  h_ref, hn_ref, route_ref, cnt_ref, carry_sc):
    i = pl.program_id(0)

    @pl.when(i == 0)
    def _():
        carry_sc[...] = jnp.zeros_like(carry_sc)

    za = _dot(ya_ref[...], wa_ref[...])
    zb = _dot(yb_ref[...], wb_ref[...])
    z = ga_ref[...].astype(F32) * za + gb_ref[...].astype(F32) * zb
    h = x_ref[...] + _dot(z.astype(BF16), wo_ref[...])
    h_ref[...] = h
    ms = jnp.mean(h * h, axis=-1, keepdims=True)
    hn = h * lax.rsqrt(ms + NORM_EPS) * fw_ref[...]
    half = hn.shape[1] // 2
    hn_ref[...] = _pack_bf16_pair(hn[:, :half], hn[:, half:])

    logits = _dot_f32x(hn, wr_hi_ref[...], wr_lo_ref[...]) + br_ref[...]
    tm = logits.shape[0]
    lane = lax.broadcasted_iota(jnp.int32, (tm, LANES), 1).astype(F32)
    work = logits
    vals, idxs = [], []
    for _ in range(TOP_K):
        m = jnp.max(work, axis=-1, keepdims=True)
        ix = jnp.min(jnp.where(work == m, lane, float(LANES)), axis=-1, keepdims=True)
        vals.append(m)
        idxs.append(ix)
        work = jnp.where(lane == ix, -jnp.inf, work)
    es = [jnp.exp(v - vals[0]) for v in vals]
    denom = functools.reduce(jnp.add, es)
    onehots = [(lane == ix).astype(F32) for ix in idxs]
    cnt = functools.reduce(jnp.add, onehots)
    before = _dot(tri_ref[...], cnt.astype(BF16)) + carry_sc[...]
    route = jnp.zeros((tm, LANES), F32)
    for k in range(TOP_K):
        rank = jnp.sum(onehots[k] * before, axis=-1, keepdims=True)
        route = jnp.where(lane == float(k), idxs[k], route)
        route = jnp.where(lane == float(TOP_K + k), es[k] / denom, route)
        route = jnp.where(lane == float(2 * TOP_K + k), rank, route)
    route_ref[...] = route
    total = carry_sc[...] + jnp.sum(cnt, axis=0, keepdims=True)
    carry_sc[...] = total
    cnt_ref[...] = total


def _merge(x2, ya, yb, proj, w_a, w_b, w_o, ffn_norm_w, wr_hi, wr_lo, br_pad):
    n, d = x2.shape
    tm = TM_MERGE
    tri = (jnp.arange(tm)[:, None] > jnp.arange(tm)[None, :]).astype(BF16)
    ga_blk = 3 * WIDTH * 2 // d
    const = lambda i: (0, 0)
    return pl.pallas_call(
        _merge_kernel,
        name="merge",
        grid=(n // tm,),
        in_specs=[
            pl.BlockSpec((tm, d), lambda i: (i, 0)),
            pl.BlockSpec((tm, WIDTH), lambda i: (i, 0)),
            pl.BlockSpec((tm, WIDTH), lambda i: (i, 0)),
            pl.BlockSpec((tm, d), lambda i: (i, ga_blk)),
            pl.BlockSpec((tm, d), lambda i: (i, ga_blk + 1)),
            pl.BlockSpec((WIDTH, d), const, pipeline_mode=pl.Buffered(1)),
            pl.BlockSpec((WIDTH, d), const, pipeline_mode=pl.Buffered(1)),
            pl.BlockSpec((d, d), const, pipeline_mode=pl.Buffered(1)),
            pl.BlockSpec((1, d), const),
            pl.BlockSpec((d, LANES), const),
            pl.BlockSpec((d, LANES), const),
            pl.BlockSpec((1, LANES), const),
            pl.BlockSpec((tm, tm), const),
        ],
        out_specs=[
            pl.BlockSpec((tm, d), lambda i: (i, 0)),
            pl.BlockSpec((tm, d // 2), lambda i: (i, 0)),
            pl.BlockSpec((tm, LANES), lambda i: (i, 0)),
            pl.BlockSpec((1, LANES), const),
        ],
        out_shape=[
            jax.ShapeDtypeStruct((n, d), F32),
            jax.ShapeDtypeStruct((n, d // 2), U32),
            jax.ShapeDtypeStruct((n, LANES), F32),
            jax.ShapeDtypeStruct((1, LANES), F32),
        ],
        scratch_shapes=[pltpu.VMEM((1, LANES), F32)],
        compiler_params=pltpu.CompilerParams(
            dimension_semantics=("arbitrary",), vmem_limit_bytes=VMEM_LIMIT),
    )(x2, ya, yb, proj, proj, w_a, w_b, w_o, ffn_norm_w.reshape(1, d), wr_hi, wr_lo, br_pad, tri)


def _dispatch_kernel(cnt_ref, dest_ref, hn_ref, xs_ref, zero_sc, sem, zsem, *, cap):
    i = pl.program_id(0)

    def row_copy(t, k):
        return pltpu.make_async_copy(hn_ref.at[pl.ds(t, 1)], xs_ref.at[pl.ds(dest_ref[t * TOP_K + k], 1)], sem)

    def issue(t, carry):
        for k in range(TOP_K):
            row_copy(t, k).start(priority=k % 2)
        return carry

    lax.fori_loop(0, TM_ROUTE, issue, 0, unroll=8)

    @pl.when(i == pl.num_programs(0) - 1)
    def _():
        zero_sc[...] = jnp.zeros(zero_sc.shape, zero_sc.dtype)

        def zero_fill(e, start):
            first = e * cap + cnt_ref[e]
            aligned = pl.multiple_of((first + SUBLANES - 1) // SUBLANES * SUBLANES, SUBLANES)
            for r in range(SUBLANES - 1):
                row = pltpu.make_async_copy(zero_sc.at[pl.ds(0, 1)], xs_ref.at[pl.ds(first + r, 1)], zsem)
                pl.when(first + r < aligned)(row.start if start else row.wait)
            block = pltpu.make_async_copy(zero_sc, xs_ref.at[pl.ds(aligned, TC_MOE)], zsem)
            block.start() if start else block.wait()

        for e in range(N_EXPERTS):
            zero_fill(e, True)
        for e in range(N_EXPERTS):
            zero_fill(e, False)

    def drain(t, carry):
        for k in range(TOP_K):
            row_copy(t, k).wait()
        return carry

    lax.fori_loop(0, TM_ROUTE, drain, 0, unroll=8)


def _dispatch(counts, dest_flat, hn, cap):
    n, d = hn.shape
    grid_spec = pltpu.PrefetchScalarGridSpec(
        num_scalar_prefetch=1,
        grid=(n // TM_ROUTE,),
        in_specs=[
            pl.BlockSpec((TM_ROUTE * TOP_K,), lambda i, cnt: (i,), memory_space=pltpu.SMEM),
            pl.BlockSpec((TM_ROUTE, d), lambda i, cnt: (i, 0)),
        ],
        out_specs=pl.BlockSpec(memory_space=pl.ANY),
        scratch_shapes=[pltpu.VMEM((TC_MOE, d), hn.dtype), pltpu.SemaphoreType.DMA, pltpu.SemaphoreType.DMA],
    )
    return pl.pallas_call(
        functools.partial(_dispatch_kernel, cap=cap),
        name="dispatch",
        grid_spec=grid_spec,
        out_shape=jax.ShapeDtypeStruct((N_EXPERTS * cap, d), hn.dtype),
        compiler_params=pltpu.CompilerParams(
            dimension_semantics=("arbitrary",), vmem_limit_bytes=VMEM_LIMIT),
    )(counts, dest_flat, hn)


def _moe_kernel(br_ref, be_ref, nc_ref, nb_ref, x_ref, wg_ref, wu_ref, wd_ref, bg_ref, bu_ref, bd_ref, o_ref,
                x_sc, acc_sc):
    i = pl.program_id(0)
    j = pl.program_id(1)
    d = acc_sc.shape[1]
    half = d // 2

    def process(off, m):
        rows = pl.ds(off, m)

        @pl.when(j == 0)
        def _():
            lo, hi = _unpack_bf16_pair(x_ref[rows, :])
            x_sc[rows, :half] = lo.astype(BF16)
            x_sc[rows, half:] = hi.astype(BF16)
            acc_sc[rows, :] = jnp.broadcast_to(bd_ref[...], (m, d))

        x = x_sc[rows, :]
        g = _dot(x, wg_ref[...].astype(BF16)) + bg_ref[...]
        u = _dot(x, wu_ref[...].astype(BF16)) + bu_ref[...]
        g = jnp.minimum(g, SWIGLU_LIMIT)
        u = jnp.clip(u, -SWIGLU_LIMIT, SWIGLU_LIMIT)
        hb = (u + 1.0) * (g * jax.nn.sigmoid(SWIGLU_ALPHA * g))
        acc_sc[rows, :] += _dot(hb.astype(BF16), wd_ref[...].astype(BF16))

        @pl.when(j == pl.num_programs(1) - 1)
        def _():
            y = acc_sc[rows, :]
            o_ref[rows, :] = _pack_bf16_pair(y[:, :half], y[:, half:])

    @pl.when(i < nb_ref[0])
    def _():
        n_chunks = nc_ref[i]
        pl.when(n_chunks == 4)(lambda: process(0, 4 * TC_MOE))
        pl.when(jnp.logical_or(n_chunks == 2, n_chunks == 3))(lambda: process(0, 2 * TC_MOE))
        pl.when(jnp.logical_or(n_chunks == 1, n_chunks == 3))(
            lambda: process(pl.multiple_of((n_chunks - 1) * TC_MOE, TC_MOE), TC_MOE))


def _moe(xs, blk_row, blk_expert, blk_chunks, n_blocks, w_gate_up, b_gate_up, w_down, b_down, max_blocks):
    rows, half = xs.shape
    d = 2 * half
    assert TM_MOE == 4 * TC_MOE
    nj = D_EXPERT // TF_MOE
    b_gu = b_gate_up.reshape(N_EXPERTS, 1, 2 * D_EXPERT)
    b_dn = b_down.reshape(N_EXPERTS, 1, d)

    def jj(i, j, nb):
        return jnp.where(i < nb[0], j, nj - 1)

    grid_spec = pltpu.PrefetchScalarGridSpec(
        num_scalar_prefetch=4,
        grid=(max_blocks, nj),
        in_specs=[
            pl.BlockSpec((TM_MOE, half), lambda i, j, br, be, nc, nb: (br[i], 0)),
            pl.BlockSpec((None, d, TF_MOE), lambda i, j, br, be, nc, nb: (be[i], 0, jj(i, j, nb))),
            pl.BlockSpec((None, d, TF_MOE), lambda i, j, br, be, nc, nb: (be[i], 0, jj(i, j, nb) + nj)),
            pl.BlockSpec((None, TF_MOE, d), lambda i, j, br, be, nc, nb: (be[i], jj(i, j, nb), 0)),
            pl.BlockSpec((None, 1, TF_MOE), lambda i, j, br, be, nc, nb: (be[i], 0, jj(i, j, nb))),
            pl.BlockSpec((None, 1, TF_MOE), lambda i, j, br, be, nc, nb: (be[i], 0, jj(i, j, nb) + nj)),
            pl.BlockSpec((None, 1, d), lambda i, j, br, be, nc, nb: (be[i], 0, 0)),
        ],
        out_specs=pl.BlockSpec((TM_MOE, half), lambda i, j, br, be, nc, nb: (br[i], 0)),
        scratch_shapes=[pltpu.VMEM((TM_MOE, d), BF16), pltpu.VMEM((TM_MOE, d), F32)],
    )
    return pl.pallas_call(
        _moe_kernel,
        name="moe",
        grid_spec=grid_spec,
        out_shape=jax.ShapeDtypeStruct((rows, half), U32),
        compiler_params=pltpu.CompilerParams(
            dimension_semantics=("arbitrary", "arbitrary"), vmem_limit_bytes=VMEM_LIMIT_MOE),
    )(blk_row, blk_expert, blk_chunks, n_blocks, xs, w_gate_up, w_gate_up, w_down, b_gu, b_gu, b_dn)


def _combine_kernel(dest_ref, h_ref, route_ref, ys_ref, o_ref, buf, sem):
    def row_copy(t, k):
        return pltpu.make_async_copy(ys_ref.at[pl.ds(dest_ref[t * TOP_K + k], 1)], buf.at[k, pl.ds(t, 1)], sem)

    def issue(t, carry):
        for k in range(TOP_K):
            row_copy(t, k).start(priority=k % 2)
        return carry

    def drain(t, carry):
        for k in range(TOP_K):
            row_copy(t, k).wait()
        return carry

    lax.fori_loop(0, TM_ROUTE, issue, 0, unroll=8)
    lax.fori_loop(0, TM_ROUTE, drain, 0, unroll=8)
    route = route_ref[...]
    half = buf.shape[2]
    acc_lo = h_ref[:, :half]
    acc_hi = h_ref[:, half:]
    for k in range(TOP_K):
        gate = route[:, TOP_K + k:TOP_K + k + 1]
        lo, hi = _unpack_bf16_pair(buf[k])
        acc_lo = acc_lo + gate * lo
        acc_hi = acc_hi + gate * hi
    o_ref[:, :half] = acc_lo
    o_ref[:, half:] = acc_hi


def _combine(dest_flat, h, route, ys):
    n, d = h.shape
    return pl.pallas_call(
        _combine_kernel,
        name="combine",
        grid=(n // TM_ROUTE,),
        in_specs=[
            pl.BlockSpec((TM_ROUTE * TOP_K,), lambda i: (i,), memory_space=pltpu.SMEM),
            pl.BlockSpec((TM_ROUTE, d), lambda i: (i, 0)),
            pl.BlockSpec((TM_ROUTE, LANES), lambda i: (i, 0)),
            pl.BlockSpec(memory_space=pl.ANY),
        ],
        out_specs=pl.BlockSpec((TM_ROUTE, d), lambda i: (i, 0)),
        out_shape=jax.ShapeDtypeStruct((n, d), F32),
        scratch_shapes=[pltpu.VMEM((TOP_K, TM_ROUTE, d // 2), U32), pltpu.SemaphoreType.DMA],
        compiler_params=pltpu.CompilerParams(
            dimension_semantics=("arbitrary",), vmem_limit_bytes=VMEM_LIMIT),
    )(dest_flat, h, route, ys)


def _pad_lanes(a):
    return jnp.pad(a, ((0, 0), (0, LANES - a.shape[1])))


def kernel(x, attn_norm_w, w_in, b_forget, qn_a, kn_a, qn_b, kn_b, rel_bias, w_branch_a, w_branch_b,
           w_out, ffn_norm_w, w_router, b_router, w_gate_up, b_gate_up, w_down, b_down):
    b, s, d = x.shape
    n = b * s
    x2 = x.reshape(n, d)

    n_qkv = 6 * WIDTH
    w_all = jnp.concatenate([w_in[:, :n_qkv], w_in[:, n_qkv + N_HEADS:]], axis=1).astype(BF16)
    bf_pad = _pad_lanes(b_forget.reshape(1, N_HEADS))
    scale = HEAD_DIM ** -0.5
    ones = jnp.ones((HEAD_DIM,), F32)
    head_norm_w = jnp.stack([qn_a * scale, kn_a, ones, qn_b * scale, kn_b, ones, ones, ones])
    wr = _pad_lanes(w_router)
    wr_hi = wr.astype(BF16)
    wr_lo = (wr - wr_hi.astype(F32)).astype(BF16)
    br_pad = jnp.concatenate([b_router.reshape(1, N_EXPERTS),
                              jnp.full((1, LANES - N_EXPERTS), NEG_INF, F32)], axis=1)

    proj, cum = _in_proj(x2, attn_norm_w, w_all, w_in, bf_pad, head_norm_w, s)
    proj3 = proj.reshape(b, s, proj.shape[1])

    y_a = _attn_a(proj3, _attn_a_ext_table(rel_bias))
    cum3 = cum.reshape(b, s, LANES)
    cum_rows = jnp.transpose(cum3[:, :, :N_HEADS], (0, 2, 1)).reshape(b, N_HEADS, s // TK_B, 1, TK_B)
    y_b = _attn_b(proj3, cum3, cum_rows)

    h, hn, route, cnt = _merge(x2, y_a.reshape(n, WIDTH), y_b.reshape(n, WIDTH), proj,
                                w_branch_a.astype(BF16), w_branch_b.astype(BF16), w_out.astype(BF16),
                                ffn_norm_w, wr_hi, wr_lo, br_pad)

    cap = n + TM_MOE
    blocks_per_expert = cap // TM_MOE
    top_e = route[:, :TOP_K].astype(jnp.int32)
    rank = route[:, 2 * TOP_K:3 * TOP_K].astype(jnp.int32)
    dest_flat = (top_e * cap + rank).reshape(n * TOP_K)
    counts = cnt[0, :N_EXPERTS].astype(jnp.int32)
    nblk = (counts + TM_MOE - 1) // TM_MOE
    bend = jnp.cumsum(nblk)
    bstart = bend - nblk
    max_blocks = n * TOP_K // TM_MOE + N_EXPERTS
    blk = jnp.minimum(jnp.arange(max_blocks), bend[-1] - 1)
    blk_expert = jnp.minimum(jnp.sum(blk[:, None] >= bend[None, :], axis=1), N_EXPERTS - 1).astype(jnp.int32)
    blk_in_expert = blk - bstart[blk_expert]
    blk_row = (blk_expert * blocks_per_expert + blk_in_expert).astype(jnp.int32)
    blk_tokens = jnp.clip(counts[blk_expert] - blk_in_expert * TM_MOE, 0, TM_MOE)
    blk_chunks = ((blk_tokens + TC_MOE - 1) // TC_MOE).astype(jnp.int32)
    n_blocks = bend[-1].astype(jnp.int32).reshape(1)

    xs = _dispatch(counts, dest_flat, hn, cap)
    ys = _moe(xs, blk_row, blk_expert, blk_chunks, n_blocks, w_gate_up, b_gate_up, w_down, b_down, max_blocks)
    out = _combine(dest_flat, h, route, ys)
    return out.reshape(b, s, d)
```

```python
import functools

import jax
import jax.numpy as jnp
from jax import lax
from jax.experimental import pallas as pl
from jax.experimental.pallas import tpu as pltpu

D_MODEL = 2048
CHUNK = 64
LEFT_CHUNKS = 8
BAND = (LEFT_CHUNKS + 1) * CHUNK
HEAD_DIM = 128
N_HEADS = 8
WIDTH = N_HEADS * HEAD_DIM
REL_CLIP = 256
N_EXPERTS = 32
TOP_K = 4
D_EXPERT = D_MODEL
SWIGLU_LIMIT = 7.0
SWIGLU_ALPHA = 1.702
NORM_EPS = 1e-5
NEG_INF = -1e30

LANES = 128
SUBLANES = 8
VMEM_LIMIT = 56 * 1024 * 1024
VMEM_LIMIT_MOE = 60 * 1024 * 1024

TM_PROJ = 1024
TN_PROJ = 1024
TQ_A = 256
WIN_A = TQ_A + LEFT_CHUNKS * CHUNK
TQ_B = 512
TK_B = 512
HB_B = 4
TM_MERGE = 256
TM_ROUTE = 256
TM_MOE = 1024
TC_MOE = 256
TF_MOE = 512

F32 = jnp.float32
BF16 = jnp.bfloat16
U32 = jnp.uint32


def _dot(a, b):
    return jnp.dot(a, b, preferred_element_type=F32)


def _dot_nt(a, b):
    return lax.dot_general(a, b, (((1,), (1,)), ((), ())), preferred_element_type=F32)


def _pack_bf16_pair(lo, hi):
    lo_bits = lax.bitcast_convert_type(lo.astype(BF16).astype(F32), U32)
    hi_bits = lax.bitcast_convert_type(hi.astype(BF16).astype(F32), U32)
    return (lo_bits >> 16) | hi_bits


def _unpack_bf16_pair(w):
    lo = lax.bitcast_convert_type(w << 16, F32)
    hi = lax.bitcast_convert_type(w & jnp.uint32(0xFFFF0000), F32)
    return lo, hi


def _split3(x):
    hi = x.astype(BF16)
    r = x - hi.astype(F32)
    mid = r.astype(BF16)
    lo = (r - mid.astype(F32)).astype(BF16)
    return hi, mid, lo


def _dot_f32x(x, w_hi, w_lo):
    hi, mid, _ = _split3(x)
    return _dot(hi, w_hi) + (_dot(hi, w_lo) + _dot(mid, w_hi))


def _w_prep_kernel(a_ref, b_ref, o_ref, *, n_plain):
    j = pl.program_id(0)

    @pl.when(j < n_plain)
    def _():
        o_ref[...] = a_ref[...].astype(BF16)

    @pl.when(j >= n_plain)
    def _():
        tn = a_ref.shape[1]
        ra = pltpu.roll(a_ref[...], tn - N_HEADS, 1)
        rb = pltpu.roll(b_ref[...], LANES - N_HEADS, 1)
        lane = lax.broadcasted_iota(jnp.int32, rb.shape, 1)
        tail = jnp.where(lane < LANES - N_HEADS, ra[:, tn - LANES:], rb)
        o_ref[:, :tn - LANES] = ra[:, :tn - LANES].astype(BF16)
        o_ref[:, tn - LANES:] = tail.astype(BF16)


def _w_prep(w_in):
    d, cols = w_in.shape
    n_out = (cols - N_HEADS) // TN_PROJ
    n_plain = 6 * WIDTH // TN_PROJ
    return pl.pallas_call(
        functools.partial(_w_prep_kernel, n_plain=n_plain),
        name="w_prep",
        grid=(n_out,),
        in_specs=[
            pl.BlockSpec((d, TN_PROJ), lambda j: (0, j)),
            pl.BlockSpec((d, LANES), lambda j: (0, (j + 1) * (TN_PROJ // LANES))),
        ],
        out_specs=pl.BlockSpec((d, TN_PROJ), lambda j: (0, j)),
        out_shape=jax.ShapeDtypeStruct((d, n_out * TN_PROJ), BF16),
        compiler_params=pltpu.CompilerParams(
            dimension_semantics=("arbitrary",), vmem_limit_bytes=VMEM_LIMIT),
    )(w_in, w_in)


def _in_proj_kernel(x_ref, nw_ref, w_ref, wf_ref, bf_ref, hn_ref, tri_ref,
                    out_ref, cum_ref, xn_sc, carry_sc, *, tiles_per_seq):
    i = pl.program_id(0)
    j = pl.program_id(1)

    @pl.when(j == 0)
    def _():
        x = x_ref[...]
        ms = jnp.mean(x * x, axis=-1, keepdims=True)
        xn = x * lax.rsqrt(ms + NORM_EPS) * nw_ref[...]
        xn_sc[...] = xn.astype(BF16)
        lane = lax.broadcasted_iota(jnp.int32, wf_ref.shape, 1)
        wf = jnp.where(lane < N_HEADS, wf_ref[...], 0.0)
        wf_hi = wf.astype(BF16)
        wf_lo = (wf - wf_hi.astype(F32)).astype(BF16)
        z = _dot_f32x(xn, wf_hi, wf_lo) + bf_ref[...]
        logf = jnp.minimum(z, 0.0) - jnp.log1p(jnp.exp(-jnp.abs(z)))
        hi, mid, lo = _split3(logf)
        tri = tri_ref[...]
        c = _dot(tri, hi) + (_dot(tri, mid) + _dot(tri, lo))
        carry = jnp.where(i % tiles_per_seq == 0, 0.0, carry_sc[...])
        c = c + carry
        cum_ref[...] = c
        carry_sc[...] = c[-1:, :]

    acc = _dot(xn_sc[...], w_ref[...])
    is_norm = jnp.logical_and(j < 6, jnp.logical_and(j != 2, j != 5))

    @pl.when(is_norm)
    def _():
        w = hn_ref[pl.ds(j, 1), :]
        for h in range(N_HEADS):
            sl = slice(h * HEAD_DIM, (h + 1) * HEAD_DIM)
            t = acc[:, sl]
            ms = jnp.mean(t * t, axis=-1, keepdims=True)
            out_ref[:, sl] = (t * lax.rsqrt(ms + NORM_EPS) * w).astype(BF16)

    @pl.when(jnp.logical_or(j == 2, j == 5))
    def _():
        out_ref[...] = acc.astype(BF16)

    @pl.when(j >= 6)
    def _():
        out_ref[...] = jax.nn.sigmoid(acc).astype(BF16)


def _in_proj(x2, attn_norm_w, w_all, w_in, bf_pad, head_norm_w, seq, tm=TM_PROJ, name="in_proj"):
    n, d = x2.shape
    n_col = w_all.shape[1] // TN_PROJ
    tri = (jnp.arange(tm)[:, None] >= jnp.arange(tm)[None, :]).astype(BF16)
    return pl.pallas_call(
        functools.partial(_in_proj_kernel, tiles_per_seq=seq // tm),
        name=name,
        grid=(n // tm, n_col),
        in_specs=[
            pl.BlockSpec((tm, d), lambda i, j: (i, 0)),
            pl.BlockSpec((1, d), lambda i, j: (0, 0)),
            pl.BlockSpec((d, TN_PROJ), lambda i, j: (0, j)),
            pl.BlockSpec((d, LANES), lambda i, j: (0, 6 * WIDTH // LANES)),
            pl.BlockSpec((1, LANES), lambda i, j: (0, 0)),
            pl.BlockSpec(head_norm_w.shape, lambda i, j: (0, 0)),
            pl.BlockSpec((tm, tm), lambda i, j: (0, 0)),
        ],
        out_specs=[
            pl.BlockSpec((tm, TN_PROJ), lambda i, j: (i, j)),
            pl.BlockSpec((tm, LANES), lambda i, j: (i, 0)),
        ],
        out_shape=[
            jax.ShapeDtypeStruct((n, w_all.shape[1]), BF16),
            jax.ShapeDtypeStruct((n, LANES), F32),
        ],
        scratch_shapes=[pltpu.VMEM((tm, d), BF16), pltpu.VMEM((1, LANES), F32)],
        compiler_params=pltpu.CompilerParams(
            dimension_semantics=("arbitrary", "arbitrary"), vmem_limit_bytes=VMEM_LIMIT),
    )(x2, attn_norm_w.reshape(1, d), w_all, w_in, bf_pad, head_norm_w, tri)


def _attn_a_kernel(q_ref, k0_ref, k1_ref, k2_ref, v0_ref, v1_ref, v2_ref, ext_ref, o_ref, bias_ref):
    qb = pl.program_id(1)

    @pl.when(jnp.logical_and(pl.program_id(0) == 0, qb == 0))
    def _():
        ext_len = ext_ref.shape[1]
        r = lax.broadcasted_iota(jnp.int32, (TQ_A, WIN_A), 0)
        c = lax.broadcasted_iota(jnp.int32, (TQ_A, WIN_A), 1)
        lo = (r // CHUNK) * CHUNK
        in_band = jnp.logical_and(c >= lo, c < lo + BAND)
        for h in range(N_HEADS):
            rows = jnp.broadcast_to(ext_ref[pl.ds(h, 1), :], (TQ_A, ext_len))
            win = pltpu.roll(rows, ext_len - (TQ_A - 1), 1, stride=1, stride_axis=0)[:, :WIN_A]
            bias_ref[h] = jnp.where(in_band, win, NEG_INF)
    k_refs = (k0_ref, k1_ref, k2_ref)
    v_refs = (v0_ref, v1_ref, v2_ref)
    n_win = len(k_refs)
    for h in range(N_HEADS):
        sl = slice(h * HEAD_DIM, (h + 1) * HEAD_DIM)
        q = q_ref[:, sl]
        scores = []
        for t in range(n_win):
            s = _dot_nt(q, k_refs[t][:, sl]) + bias_ref[h, :, t * TQ_A:(t + 1) * TQ_A]
            if t < n_win - 1:
                s = jnp.where(qb >= n_win - 1 - t, s, NEG_INF)
            scores.append(s)
        m = functools.reduce(jnp.maximum, [jnp.max(s, axis=-1, keepdims=True) for s in scores])
        ps = [jnp.exp(s - m) for s in scores]
        l = functools.reduce(jnp.add, [jnp.sum(p, axis=-1, keepdims=True) for p in ps])
        o = functools.reduce(jnp.add, [_dot(ps[t].astype(BF16), v_refs[t][:, sl]) for t in range(n_win)])
        o_ref[:, sl] = (o / l).astype(BF16)


def _attn_a_ext_table(rel_bias):
    j = jnp.arange(TQ_A + WIN_A)
    rel = (TQ_A - 1 + LEFT_CHUNKS * CHUNK) - j
    return rel_bias[:, jnp.clip(rel, -(CHUNK - 1), REL_CLIP) + (CHUNK - 1)].astype(F32)


def _attn_a(proj3, ext):
    b, s, _ = proj3.shape
    nq = s // TQ_A
    n_win = WIN_A // TQ_A

    def kv_spec(group, t):
        return pl.BlockSpec((None, TQ_A, WIDTH),
                            lambda bi, qi: (bi, jnp.maximum(qi - (n_win - 1) + t, 0), group))

    return pl.pallas_call(
        _attn_a_kernel,
        name="attn_a",
        grid=(b, nq),
        in_specs=[pl.BlockSpec((None, TQ_A, WIDTH), lambda bi, qi: (bi, qi, 0))]
        + [kv_spec(1, t) for t in range(n_win)]
        + [kv_spec(2, t) for t in range(n_win)]
        + [pl.BlockSpec(ext.shape, lambda bi, qi: (0, 0))],
        out_specs=pl.BlockSpec((None, TQ_A, WIDTH), lambda bi, qi: (bi, qi, 0)),
        out_shape=jax.ShapeDtypeStruct((b, s, WIDTH), BF16),
        scratch_shapes=[pltpu.VMEM((N_HEADS, TQ_A, WIN_A), F32)],
        compiler_params=pltpu.CompilerParams(
            dimension_semantics=("arbitrary", "arbitrary"), vmem_limit_bytes=VMEM_LIMIT),
    )(proj3, *([proj3] * (2 * n_win)), ext)


def _attn_b_kernel(q_ref, k_ref, v_ref, cq_ref, ck_ref, o_ref, m_sc, l_sc, acc_sc, *, hb):
    hp = pl.program_id(1)
    qi = pl.program_id(2)
    n_chunk = TK_B // LANES
    lane = lax.broadcasted_iota(jnp.int32, (TQ_B, LANES), 1)
    row = lax.broadcasted_iota(jnp.int32, (TQ_B, LANES), 0)
    qs, cqs = [], []
    for a in range(hb):
        qs.append(q_ref[:, a * HEAD_DIM:(a + 1) * HEAD_DIM])
        cq = jnp.sum(jnp.where(lane == hp * hb + a, cq_ref[...], 0.0), axis=-1, keepdims=True)
        cqs.append(jnp.broadcast_to(cq, (TQ_B, LANES)))
    m_sc[...] = jnp.full(m_sc.shape, NEG_INF, F32)
    l_sc[...] = jnp.zeros(l_sc.shape, F32)
    acc_sc[...] = jnp.zeros(acc_sc.shape, F32)

    def block(kb, masked):
        for a in range(hb):
            sl = slice(a * HEAD_DIM, (a + 1) * HEAD_DIM)
            k = k_ref[pl.ds(kb * TK_B, TK_B), sl]
            v = v_ref[pl.ds(kb * TK_B, TK_B), sl]
            s = _dot_nt(qs[a], k)
            ck = ck_ref[a, kb]
            chunks = []
            for c in range(n_chunk):
                cs = slice(c * LANES, (c + 1) * LANES)
                sc = s[:, cs] + (cqs[a] - ck[:, cs])
                if masked:
                    sc = jnp.where(lane + c * LANES <= row, sc, NEG_INF)
                chunks.append(sc)
            m_old = m_sc[a]
            m_new = jnp.maximum(m_old, jnp.max(functools.reduce(jnp.maximum, chunks), axis=-1, keepdims=True))
            alpha = jnp.exp(m_old - m_new)
            ps = [jnp.exp(sc - m_new) for sc in chunks]
            l_sc[a] = alpha * l_sc[a] + jnp.sum(functools.reduce(jnp.add, ps), axis=-1, keepdims=True)
            p = jnp.concatenate([pc.astype(BF16) for pc in ps], axis=1)
            acc_sc[a] = alpha * acc_sc[a] + _dot(p, v)
            m_sc[a] = m_new

    def body(kb, carry):
        block(kb, False)
        return carry

    lax.fori_loop(0, qi, body, 0)
    block(qi, True)
    for a in range(hb):
        o_ref[:, a * HEAD_DIM:(a + 1) * HEAD_DIM] = (acc_sc[a] / l_sc[a]).astype(BF16)


def _attn_b(proj3, cum3, cum_rows, hb=HB_B, name="attn_b"):
    b, s, _ = proj3.shape
    nq = s // TQ_B
    wb = hb * HEAD_DIM
    q0 = 3 * WIDTH // wb
    gstep = WIDTH // wb
    return pl.pallas_call(
        functools.partial(_attn_b_kernel, hb=hb),
        name=name,
        grid=(b, N_HEADS // hb, nq),
        in_specs=[
            pl.BlockSpec((None, TQ_B, wb), lambda bi, hp, qi: (bi, qi, q0 + hp)),
            pl.BlockSpec((None, s, wb), lambda bi, hp, qi: (bi, 0, q0 + gstep + hp)),
            pl.BlockSpec((None, s, wb), lambda bi, hp, qi: (bi, 0, q0 + 2 * gstep + hp)),
            pl.BlockSpec((None, TQ_B, LANES), lambda bi, hp, qi: (bi, qi, 0)),
            pl.BlockSpec((None, hb, s // TK_B, 1, TK_B), lambda bi, hp, qi: (bi, hp, 0, 0, 0)),
        ],
        out_specs=pl.BlockSpec((None, TQ_B, wb), lambda bi, hp, qi: (bi, qi, hp)),
        out_shape=jax.ShapeDtypeStruct((b, s, WIDTH), BF16),
        scratch_shapes=[pltpu.VMEM((hb, TQ_B, LANES), F32), pltpu.VMEM((hb, TQ_B, LANES), F32),
                        pltpu.VMEM((hb, TQ_B, HEAD_DIM), F32)],
        compiler_params=pltpu.CompilerParams(
            dimension_semantics=("arbitrary", "arbitrary", "arbitrary"), vmem_limit_bytes=VMEM_LIMIT),
    )(proj3, proj3, proj3, cum3, cum_rows)


def _merge_kernel(x_ref, ya_ref, yb_ref, ga_ref, gb_ref, wa_ref, wb_ref, wo_ref, fw_ref,
                  wr_hi_ref, wr_lo_ref, br_ref, tri_ref,
                  h_ref, hn_ref, route_ref, cnt_ref, carry_sc):
    i = pl.program_id(0)

    @pl.when(i == 0)
    def _():
        carry_sc[...] = jnp.zeros_like(carry_sc)

    za = _dot(ya_ref[...], wa_ref[...])
    zb = _dot(yb_ref[...], wb_ref[...])
    z = ga_ref[...].astype(F32) * za + gb_ref[...].astype(F32) * zb
    h = x_ref[...] + _dot(z.astype(BF16), wo_ref[...])
    h_ref[...] = h
    ms = jnp.mean(h * h, axis=-1, keepdims=True)
    hn = h * lax.rsqrt(ms + NORM_EPS) * fw_ref[...]
    half = hn.shape[1] // 2
    hn_ref[...] = _pack_bf16_pair(hn[:, :half], hn[:, half:])

    logits = _dot_f32x(hn, wr_hi_ref[...], wr_lo_ref[...]) + br_ref[...]
    tm = logits.shape[0]
    lane = lax.broadcasted_iota(jnp.int32, (tm, LANES), 1).astype(F32)
    work = logits
    vals, idxs = [], []
    for _ in range(TOP_K):
        m = jnp.max(work, axis=-1, keepdims=True)
        ix = jnp.min(jnp.where(work == m, lane, float(LANES)), axis=-1, keepdims=True)
        vals.append(m)
        idxs.append(ix)
        work = jnp.where(lane == ix, -jnp.inf, work)
    es = [jnp.exp(v - vals[0]) for v in vals]
    denom = functools.reduce(jnp.add, es)
    onehots = [(lane == ix).astype(F32) for ix in idxs]
    cnt = functools.reduce(jnp.add, onehots)
    before = _dot(tri_ref[...], cnt.astype(BF16)) + carry_sc[...]
    route = jnp.zeros((tm, LANES), F32)
    for k in range(TOP_K):
        rank = jnp.sum(onehots[k] * before, axis=-1, keepdims=True)
        route = jnp.where(lane == float(k), idxs[k], route)
        route = jnp.where(lane == float(TOP_K + k), es[k] / denom, route)
        route = jnp.where(lane == float(2 * TOP_K + k), rank, route)
    route_ref[...] = route
    total = carry_sc[...] + jnp.sum(cnt, axis=0, keepdims=True)
    carry_sc[...] = total
    cnt_ref[...] = total


def _merge(x2, ya, yb, proj, w_a, w_b, w_o, ffn_norm_w, wr_hi, wr_lo, br_pad):
    n, d = x2.shape
    tm = TM_MERGE
    tri = (jnp.arange(tm)[:, None] > jnp.arange(tm)[None, :]).astype(BF16)
    ga_blk = 3 * WIDTH * 2 // d
    const = lambda i: (0, 0)
    return pl.pallas_call(
        _merge_kernel,
        name="merge",
        grid=(n // tm,),
        in_specs=[
            pl.BlockSpec((tm, d), lambda i: (i, 0)),
            pl.BlockSpec((tm, WIDTH), lambda i: (i, 0)),
            pl.BlockSpec((tm, WIDTH), lambda i: (i, 0)),
            pl.BlockSpec((tm, d), lambda i: (i, ga_blk)),
            pl.BlockSpec((tm, d), lambda i: (i, ga_blk + 1)),
            pl.BlockSpec((WIDTH, d), const, pipeline_mode=pl.Buffered(1)),
            pl.BlockSpec((WIDTH, d), const, pipeline_mode=pl.Buffered(1)),
            pl.BlockSpec((d, d), const, pipeline_mode=pl.Buffered(1)),
            pl.BlockSpec((1, d), const),
            pl.BlockSpec((d, LANES), const),
            pl.BlockSpec((d, LANES), const),
            pl.BlockSpec((1, LANES), const),
            pl.BlockSpec((tm, tm), const),
        ],
        out_specs=[
            pl.BlockSpec((tm, d), lambda i: (i, 0)),
            pl.BlockSpec((tm, d // 2), lambda i: (i, 0)),
            pl.BlockSpec((tm, LANES), lambda i: (i, 0)),
            pl.BlockSpec((1, LANES), const),
        ],
        out_shape=[
            jax.ShapeDtypeStruct((n, d), F32),
            jax.ShapeDtypeStruct((n, d // 2), U32),
            jax.ShapeDtypeStruct((n, LANES), F32),
            jax.ShapeDtypeStruct((1, LANES), F32),
        ],
        scratch_shapes=[pltpu.VMEM((1, LANES), F32)],
        compiler_params=pltpu.CompilerParams(
            dimension_semantics=("arbitrary",), vmem_limit_bytes=VMEM_LIMIT),
    )(x2, ya, yb, proj, proj, w_a, w_b, w_o, ffn_norm_w.reshape(1, d), wr_hi, wr_lo, br_pad, tri)


def _dispatch_kernel(cnt_ref, dest_ref, hn_ref, xs_ref, zero_sc, sem, zsem, *, cap):
    i = pl.program_id(0)

    def row_copy(t, k):
        return pltpu.make_async_copy(hn_ref.at[pl.ds(t, 1)], xs_ref.at[pl.ds(dest_ref[t * TOP_K + k], 1)], sem)

    def issue(t, carry):
        for k in range(TOP_K):
            row_copy(t, k).start(priority=k % 2)
        return carry

    lax.fori_loop(0, TM_ROUTE, issue, 0, unroll=8)

    @pl.when(i == pl.num_programs(0) - 1)
    def _():
        zero_sc[...] = jnp.zeros(zero_sc.shape, zero_sc.dtype)

        def zero_fill(e, start):
            first = e * cap + cnt_ref[e]
            aligned = pl.multiple_of((first + SUBLANES - 1) // SUBLANES * SUBLANES, SUBLANES)
            for r in range(SUBLANES - 1):
                row = pltpu.make_async_copy(zero_sc.at[pl.ds(0, 1)], xs_ref.at[pl.ds(first + r, 1)], zsem)
                pl.when(first + r < aligned)(row.start if start else row.wait)
            block = pltpu.make_async_copy(zero_sc, xs_ref.at[pl.ds(aligned, TC_MOE)], zsem)
            block.start() if start else block.wait()

        for e in range(N_EXPERTS):
            zero_fill(e, True)
        for e in range(N_EXPERTS):
            zero_fill(e, False)

    def drain(t, carry):
        for k in range(TOP_K):
            row_copy(t, k).wait()
        return carry

    lax.fori_loop(0, TM_ROUTE, drain, 0, unroll=8)


def _dispatch(counts, dest_flat, hn, cap):
    n, d = hn.shape
    grid_spec = pltpu.PrefetchScalarGridSpec(
        num_scalar_prefetch=1,
        grid=(n // TM_ROUTE,),
        in_specs=[
            pl.BlockSpec((TM_ROUTE * TOP_K,), lambda i, cnt: (i,), memory_space=pltpu.SMEM),
            pl.BlockSpec((TM_ROUTE, d), lambda i, cnt: (i, 0)),
        ],
        out_specs=pl.BlockSpec(memory_space=pl.ANY),
        scratch_shapes=[pltpu.VMEM((TC_MOE, d), hn.dtype), pltpu.SemaphoreType.DMA, pltpu.SemaphoreType.DMA],
    )
    return pl.pallas_call(
        functools.partial(_dispatch_kernel, cap=cap),
        name="dispatch",
        grid_spec=grid_spec,
        out_shape=jax.ShapeDtypeStruct((N_EXPERTS * cap, d), hn.dtype),
        compiler_params=pltpu.CompilerParams(
            dimension_semantics=("arbitrary",), vmem_limit_bytes=VMEM_LIMIT),
    )(counts, dest_flat, hn)


def _moe_kernel(br_ref, be_ref, nc_ref, nb_ref, x_ref, wg_ref, wu_ref, wd_ref, bg_ref, bu_ref, bd_ref, o_ref,
                x_sc, acc_sc):
    i = pl.program_id(0)
    j = pl.program_id(1)
    d = acc_sc.shape[1]
    half = d // 2

    def process(off, m):
        rows = pl.ds(off, m)

        @pl.when(j == 0)
        def _():
            lo, hi = _unpack_bf16_pair(x_ref[rows, :])
            x_sc[rows, :half] = lo.astype(BF16)
            x_sc[rows, half:] = hi.astype(BF16)
            acc_sc[rows, :] = jnp.broadcast_to(bd_ref[...], (m, d))

        x = x_sc[rows, :]
        g = _dot(x, wg_ref[...].astype(BF16)) + bg_ref[...]
        u = _dot(x, wu_ref[...].astype(BF16)) + bu_ref[...]
        g = jnp.minimum(g, SWIGLU_LIMIT)
        u = jnp.clip(u, -SWIGLU_LIMIT, SWIGLU_LIMIT)
        hb = (u + 1.0) * (g * jax.nn.sigmoid(SWIGLU_ALPHA * g))
        acc_sc[rows, :] += _dot(hb.astype(BF16), wd_ref[...].astype(BF16))

        @pl.when(j == pl.num_programs(1) - 1)
        def _():
            y = acc_sc[rows, :]
            o_ref[rows, :] = _pack_bf16_pair(y[:, :half], y[:, half:])

    @pl.when(i < nb_ref[0])
    def _():
        n_chunks = nc_ref[i]
        pl.when(n_chunks == 4)(lambda: process(0, 4 * TC_MOE))
        pl.when(jnp.logical_or(n_chunks == 2, n_chunks == 3))(lambda: process(0, 2 * TC_MOE))
        pl.when(jnp.logical_or(n_chunks == 1, n_chunks == 3))(
            lambda: process(pl.multiple_of((n_chunks - 1) * TC_MOE, TC_MOE), TC_MOE))


def _moe(xs, blk_row, blk_expert, blk_chunks, n_blocks, w_gate_up, b_gate_up, w_down, b_down, max_blocks):
    rows, half = xs.shape
    d = 2 * half
    assert TM_MOE == 4 * TC_MOE
    nj = D_EXPERT // TF_MOE
    b_gu = b_gate_up.reshape(N_EXPERTS, 1, 2 * D_EXPERT)
    b_dn = b_down.reshape(N_EXPERTS, 1, d)

    def jj(i, j, nb):
        return jnp.where(i < nb[0], j, nj - 1)

    grid_spec = pltpu.PrefetchScalarGridSpec(
        num_scalar_prefetch=4,
        grid=(max_blocks, nj),
        in_specs=[
            pl.BlockSpec((TM_MOE, half), lambda i, j, br, be, nc, nb: (br[i], 0)),
            pl.BlockSpec((None, d, TF_MOE), lambda i, j, br, be, nc, nb: (be[i], 0, jj(i, j, nb))),
            pl.BlockSpec((None, d, TF_MOE), lambda i, j, br, be, nc, nb: (be[i], 0, jj(i, j, nb) + nj)),
            pl.BlockSpec((None, TF_MOE, d), lambda i, j, br, be, nc, nb: (be[i], jj(i, j, nb), 0)),
            pl.BlockSpec((None, 1, TF_MOE), lambda i, j, br, be, nc, nb: (be[i], 0, jj(i, j, nb))),
            pl.BlockSpec((None, 1, TF_MOE), lambda i, j, br, be, nc, nb: (be[i], 0, jj(i, j, nb) + nj)),
            pl.BlockSpec((None, 1, d), lambda i, j, br, be, nc, nb: (be[i], 0, 0)),
        ],
        out_specs=pl.BlockSpec((TM_MOE, half), lambda i, j, br, be, nc, nb: (br[i], 0)),
        scratch_shapes=[pltpu.VMEM((TM_MOE, d), BF16), pltpu.VMEM((TM_MOE, d), F32)],
    )
    return pl.pallas_call(
        _moe_kernel,
        name="moe",
        grid_spec=grid_spec,
        out_shape=jax.ShapeDtypeStruct((rows, half), U32),
        compiler_params=pltpu.CompilerParams(
            dimension_semantics=("arbitrary", "arbitrary"), vmem_limit_bytes=VMEM_LIMIT_MOE),
    )(blk_row, blk_expert, blk_chunks, n_blocks, xs, w_gate_up, w_gate_up, w_down, b_gu, b_gu, b_dn)


def _combine_kernel(dest_ref, dest_next_ref, h_ref, route_ref, ys_ref, o_ref, buf, sem):
    i = pl.program_id(0)
    slot = i % 2

    def row_copy(idx_ref, t, k, s):
        return pltpu.make_async_copy(ys_ref.at[pl.ds(idx_ref[t * TOP_K + k], 1)], buf.at[s, k, pl.ds(t, 1)],
                                     sem.at[s])

    def gather(idx_ref, s):
        def issue(t, carry):
            for k in range(TOP_K):
                row_copy(idx_ref, t, k, s).start(priority=k % 2)
            return carry
        lax.fori_loop(0, TM_ROUTE, issue, 0, unroll=8)

    pl.when(i == 0)(lambda: gather(dest_ref, slot))
    pl.when(i + 1 < pl.num_programs(0))(lambda: gather(dest_next_ref, 1 - slot))

    def drain(t, carry):
        for k in range(TOP_K):
            row_copy(dest_ref, t, k, slot).wait()
        return carry

    lax.fori_loop(0, TM_ROUTE, drain, 0, unroll=8)
    route = route_ref[...]
    half = buf.shape[3]
    acc_lo = h_ref[:, :half]
    acc_hi = h_ref[:, half:]
    for k in range(TOP_K):
        gate = route[:, TOP_K + k:TOP_K + k + 1]
        lo, hi = _unpack_bf16_pair(buf[slot, k])
        acc_lo = acc_lo + gate * lo
        acc_hi = acc_hi + gate * hi
    o_ref[:, :half] = acc_lo
    o_ref[:, half:] = acc_hi


def _combine(dest_flat, h, route, ys):
    n, d = h.shape
    n_tiles = n // TM_ROUTE
    return pl.pallas_call(
        _combine_kernel,
        name="combine",
        grid=(n_tiles,),
        in_specs=[
            pl.BlockSpec((TM_ROUTE * TOP_K,), lambda i: (i,), memory_space=pltpu.SMEM),
            pl.BlockSpec((TM_ROUTE * TOP_K,), lambda i: (jnp.minimum(i + 1, n_tiles - 1),),
                         memory_space=pltpu.SMEM),
            pl.BlockSpec((TM_ROUTE, d), lambda i: (i, 0)),
            pl.BlockSpec((TM_ROUTE, LANES), lambda i: (i, 0)),
            pl.BlockSpec(memory_space=pl.ANY),
        ],
        out_specs=pl.BlockSpec((TM_ROUTE, d), lambda i: (i, 0)),
        out_shape=jax.ShapeDtypeStruct((n, d), F32),
        scratch_shapes=[pltpu.VMEM((2, TOP_K, TM_ROUTE, d // 2), U32), pltpu.SemaphoreType.DMA((2,))],
        compiler_params=pltpu.CompilerParams(
            dimension_semantics=("arbitrary",), vmem_limit_bytes=VMEM_LIMIT),
    )(dest_flat, dest_flat, h, route, ys)


def _pad_lanes(a):
    return jnp.pad(a, ((0, 0), (0, LANES - a.shape[1])))


def kernel(x, attn_norm_w, w_in, b_forget, qn_a, kn_a, qn_b, kn_b, rel_bias, w_branch_a, w_branch_b,
           w_out, ffn_norm_w, w_router, b_router, w_gate_up, b_gate_up, w_down, b_down):
    b, s, d = x.shape
    n = b * s
    x2 = x.reshape(n, d)

    w_all = _w_prep(w_in)
    bf_pad = _pad_lanes(b_forget.reshape(1, N_HEADS))
    scale = HEAD_DIM ** -0.5
    ones = jnp.ones((HEAD_DIM,), F32)
    head_norm_w = jnp.stack([qn_a * scale, kn_a, ones, qn_b * scale, kn_b, ones, ones, ones])
    wr = _pad_lanes(w_router)
    wr_hi = wr.astype(BF16)
    wr_lo = (wr - wr_hi.astype(F32)).astype(BF16)
    br_pad = jnp.concatenate([b_router.reshape(1, N_EXPERTS),
                              jnp.full((1, LANES - N_EXPERTS), NEG_INF, F32)], axis=1)

    proj, cum = _in_proj(x2, attn_norm_w, w_all, w_in, bf_pad, head_norm_w, s)
    proj3 = proj.reshape(b, s, proj.shape[1])

    y_a = _attn_a(proj3, _attn_a_ext_table(rel_bias))
    cum3 = cum.reshape(b, s, LANES)
    cum_rows = jnp.transpose(cum3[:, :, :N_HEADS], (0, 2, 1)).reshape(b, N_HEADS, s // TK_B, 1, TK_B)
    y_b = _attn_b(proj3, cum3, cum_rows)

    h, hn, route, cnt = _merge(x2, y_a.reshape(n, WIDTH), y_b.reshape(n, WIDTH), proj,
                                w_branch_a.astype(BF16), w_branch_b.astype(BF16), w_out.astype(BF16),
                                ffn_norm_w, wr_hi, wr_lo, br_pad)

    cap = n + TM_MOE
    blocks_per_expert = cap // TM_MOE
    top_e = route[:, :TOP_K].astype(jnp.int32)
    rank = route[:, 2 * TOP_K:3 * TOP_K].astype(jnp.int32)
    dest_flat = (top_e * cap + rank).reshape(n * TOP_K)
    counts = cnt[0, :N_EXPERTS].astype(jnp.int32)
    nblk = (counts + TM_MOE - 1) // TM_MOE
    bend = jnp.cumsum(nblk)
    bstart = bend - nblk
    max_blocks = n * TOP_K // TM_MOE + N_EXPERTS
    blk = jnp.minimum(jnp.arange(max_blocks), bend[-1] - 1)
    blk_expert = jnp.minimum(jnp.sum(blk[:, None] >= bend[None, :], axis=1), N_EXPERTS - 1).astype(jnp.int32)
    blk_in_expert = blk - bstart[blk_expert]
    blk_row = (blk_expert * blocks_per_expert + blk_in_expert).astype(jnp.int32)
    blk_tokens = jnp.clip(counts[blk_expert] - blk_in_expert * TM_MOE, 0, TM_MOE)
    blk_chunks = ((blk_tokens + TC_MOE - 1) // TC_MOE).astype(jnp.int32)
    n_blocks = bend[-1].astype(jnp.int32).reshape(1)

    xs = _dispatch(counts, dest_flat, hn, cap)
    ys = _moe(xs, blk_row, blk_expert, blk_chunks, n_blocks, w_gate_up, b_gate_up, w_down, b_down, max_blocks)
    out = _combine(dest_flat, h, route, ys)
    return out.reshape(b, s, d)
```

```python
import functools

import jax
import jax.numpy as jnp
from jax import lax
from jax.experimental import pallas as pl
from jax.experimental.pallas import tpu as pltpu

D_MODEL = 2048
CHUNK = 64
LEFT_CHUNKS = 8
BAND = (LEFT_CHUNKS + 1) * CHUNK
HEAD_DIM = 128
N_HEADS = 8
WIDTH = N_HEADS * HEAD_DIM
REL_CLIP = 256
N_EXPERTS = 32
TOP_K = 4
D_EXPERT = D_MODEL
SWIGLU_LIMIT = 7.0
SWIGLU_ALPHA = 1.702
NORM_EPS = 1e-5
NEG_INF = -1e30

LANES = 128
SUBLANES = 8
VMEM_LIMIT = 56 * 1024 * 1024
VMEM_LIMIT_MOE = 60 * 1024 * 1024

TM_PROJ = 1024
TN_PROJ = 1024
TQ_A = 256
WIN_A = TQ_A + LEFT_CHUNKS * CHUNK
TQ_B = 512
TK_B = 512
HB_B = 4
TM_MERGE = 256
TM_ROUTE = 256
TM_MOE = 1280
TC_MOE = 256
TF_MOE = 512

F32 = jnp.float32
BF16 = jnp.bfloat16
U32 = jnp.uint32


def _dot(a, b):
    return jnp.dot(a, b, preferred_element_type=F32)


def _dot_nt(a, b):
    return lax.dot_general(a, b, (((1,), (1,)), ((), ())), preferred_element_type=F32)


def _pack_bf16_pair(lo, hi):
    lo_bits = lax.bitcast_convert_type(lo.astype(BF16).astype(F32), U32)
    hi_bits = lax.bitcast_convert_type(hi.astype(BF16).astype(F32), U32)
    return (lo_bits >> 16) | hi_bits


def _unpack_bf16_pair(w):
    lo = lax.bitcast_convert_type(w << 16, F32)
    hi = lax.bitcast_convert_type(w & jnp.uint32(0xFFFF0000), F32)
    return lo, hi


def _split3(x):
    hi = x.astype(BF16)
    r = x - hi.astype(F32)
    mid = r.astype(BF16)
    lo = (r - mid.astype(F32)).astype(BF16)
    return hi, mid, lo


def _dot_f32x(x, w_hi, w_lo):
    hi, mid, _ = _split3(x)
    return _dot(hi, w_hi) + (_dot(hi, w_lo) + _dot(mid, w_hi))


def _w_prep_kernel(a_ref, b_ref, o_ref, *, n_plain):
    j = pl.program_id(0)

    @pl.when(j < n_plain)
    def _():
        o_ref[...] = a_ref[...].astype(BF16)

    @pl.when(j >= n_plain)
    def _():
        tn = a_ref.shape[1]
        ra = pltpu.roll(a_ref[...], tn - N_HEADS, 1)
        rb = pltpu.roll(b_ref[...], LANES - N_HEADS, 1)
        lane = lax.broadcasted_iota(jnp.int32, rb.shape, 1)
        tail = jnp.where(lane < LANES - N_HEADS, ra[:, tn - LANES:], rb)
        o_ref[:, :tn - LANES] = ra[:, :tn - LANES].astype(BF16)
        o_ref[:, tn - LANES:] = tail.astype(BF16)


def _w_prep(w_in):
    d, cols = w_in.shape
    n_out = (cols - N_HEADS) // TN_PROJ
    n_plain = 6 * WIDTH // TN_PROJ
    return pl.pallas_call(
        functools.partial(_w_prep_kernel, n_plain=n_plain),
        name="w_prep",
        grid=(n_out,),
        in_specs=[
            pl.BlockSpec((d, TN_PROJ), lambda j: (0, j)),
            pl.BlockSpec((d, LANES), lambda j: (0, (j + 1) * (TN_PROJ // LANES))),
        ],
        out_specs=pl.BlockSpec((d, TN_PROJ), lambda j: (0, j)),
        out_shape=jax.ShapeDtypeStruct((d, n_out * TN_PROJ), BF16),
        compiler_params=pltpu.CompilerParams(
            dimension_semantics=("arbitrary",), vmem_limit_bytes=VMEM_LIMIT),
    )(w_in, w_in)


def _in_proj_kernel(x_ref, nw_ref, w_ref, wf_ref, bf_ref, hn_ref, tri_ref,
                    out_ref, cum_ref, xn_sc, carry_sc, *, tiles_per_seq):
    i = pl.program_id(0)
    j = pl.program_id(1)

    @pl.when(j == 0)
    def _():
        x = x_ref[...]
        ms = jnp.mean(x * x, axis=-1, keepdims=True)
        xn = x * lax.rsqrt(ms + NORM_EPS) * nw_ref[...]
        xn_sc[...] = xn.astype(BF16)
        lane = lax.broadcasted_iota(jnp.int32, wf_ref.shape, 1)
        wf = jnp.where(lane < N_HEADS, wf_ref[...], 0.0)
        wf_hi = wf.astype(BF16)
        wf_lo = (wf - wf_hi.astype(F32)).astype(BF16)
        z = _dot_f32x(xn, wf_hi, wf_lo) + bf_ref[...]
        logf = jnp.minimum(z, 0.0) - jnp.log1p(jnp.exp(-jnp.abs(z)))
        hi, mid, lo = _split3(logf)
        tri = tri_ref[...]
        c = _dot(tri, hi) + (_dot(tri, mid) + _dot(tri, lo))
        carry = jnp.where(i % tiles_per_seq == 0, 0.0, carry_sc[...])
        c = c + carry
        cum_ref[...] = c
        carry_sc[...] = c[-1:, :]

    acc = _dot(xn_sc[...], w_ref[...])
    is_norm = jnp.logical_and(j < 6, jnp.logical_and(j != 2, j != 5))

    @pl.when(is_norm)
    def _():
        w = hn_ref[pl.ds(j, 1), :]
        for h in range(N_HEADS):
            sl = slice(h * HEAD_DIM, (h + 1) * HEAD_DIM)
            t = acc[:, sl]
            ms = jnp.mean(t * t, axis=-1, keepdims=True)
            out_ref[:, sl] = (t * lax.rsqrt(ms + NORM_EPS) * w).astype(BF16)

    @pl.when(jnp.logical_or(j == 2, j == 5))
    def _():
        out_ref[...] = acc.astype(BF16)

    @pl.when(j >= 6)
    def _():
        out_ref[...] = jax.nn.sigmoid(acc).astype(BF16)


def _in_proj(x2, attn_norm_w, w_all, w_in, bf_pad, head_norm_w, seq, tm=TM_PROJ, name="in_proj"):
    n, d = x2.shape
    n_col = w_all.shape[1] // TN_PROJ
    tri = (jnp.arange(tm)[:, None] >= jnp.arange(tm)[None, :]).astype(BF16)
    return pl.pallas_call(
        functools.partial(_in_proj_kernel, tiles_per_seq=seq // tm),
        name=name,
        grid=(n // tm, n_col),
        in_specs=[
            pl.BlockSpec((tm, d), lambda i, j: (i, 0)),
            pl.BlockSpec((1, d), lambda i, j: (0, 0)),
            pl.BlockSpec((d, TN_PROJ), lambda i, j: (0, j)),
            pl.BlockSpec((d, LANES), lambda i, j: (0, 6 * WIDTH // LANES)),
            pl.BlockSpec((1, LANES), lambda i, j: (0, 0)),
            pl.BlockSpec(head_norm_w.shape, lambda i, j: (0, 0)),
            pl.BlockSpec((tm, tm), lambda i, j: (0, 0)),
        ],
        out_specs=[
            pl.BlockSpec((tm, TN_PROJ), lambda i, j: (i, j)),
            pl.BlockSpec((tm, LANES), lambda i, j: (i, 0)),
        ],
        out_shape=[
            jax.ShapeDtypeStruct((n, w_all.shape[1]), BF16),
            jax.ShapeDtypeStruct((n, LANES), F32),
        ],
        scratch_shapes=[pltpu.VMEM((tm, d), BF16), pltpu.VMEM((1, LANES), F32)],
        compiler_params=pltpu.CompilerParams(
            dimension_semantics=("arbitrary", "arbitrary"), vmem_limit_bytes=VMEM_LIMIT),
    )(x2, attn_norm_w.reshape(1, d), w_all, w_in, bf_pad, head_norm_w, tri)


def _attn_a_kernel(q_ref, k0_ref, k1_ref, k2_ref, v0_ref, v1_ref, v2_ref, ext_ref, o_ref, bias_ref):
    qb = pl.program_id(1)

    @pl.when(jnp.logical_and(pl.program_id(0) == 0, qb == 0))
    def _():
        ext_len = ext_ref.shape[1]
        r = lax.broadcasted_iota(jnp.int32, (TQ_A, WIN_A), 0)
        c = lax.broadcasted_iota(jnp.int32, (TQ_A, WIN_A), 1)
        lo = (r // CHUNK) * CHUNK
        in_band = jnp.logical_and(c >= lo, c < lo + BAND)
        for h in range(N_HEADS):
            rows = jnp.broadcast_to(ext_ref[pl.ds(h, 1), :], (TQ_A, ext_len))
            win = pltpu.roll(rows, ext_len - (TQ_A - 1), 1, stride=1, stride_axis=0)[:, :WIN_A]
            bias_ref[h] = jnp.where(in_band, win, NEG_INF)
    k_refs = (k0_ref, k1_ref, k2_ref)
    v_refs = (v0_ref, v1_ref, v2_ref)
    n_win = len(k_refs)
    for h in range(N_HEADS):
        sl = slice(h * HEAD_DIM, (h + 1) * HEAD_DIM)
        q = q_ref[:, sl]
        scores = []
        for t in range(n_win):
            s = _dot_nt(q, k_refs[t][:, sl]) + bias_ref[h, :, t * TQ_A:(t + 1) * TQ_A]
            if t < n_win - 1:
                s = jnp.where(qb >= n_win - 1 - t, s, NEG_INF)
            scores.append(s)
        m = functools.reduce(jnp.maximum, [jnp.max(s, axis=-1, keepdims=True) for s in scores])
        ps = [jnp.exp(s - m) for s in scores]
        l = functools.reduce(jnp.add, [jnp.sum(p, axis=-1, keepdims=True) for p in ps])
        o = functools.reduce(jnp.add, [_dot(ps[t].astype(BF16), v_refs[t][:, sl]) for t in range(n_win)])
        o_ref[:, sl] = (o / l).astype(BF16)


def _attn_a_ext_table(rel_bias):
    j = jnp.arange(TQ_A + WIN_A)
    rel = (TQ_A - 1 + LEFT_CHUNKS * CHUNK) - j
    return rel_bias[:, jnp.clip(rel, -(CHUNK - 1), REL_CLIP) + (CHUNK - 1)].astype(F32)


def _attn_a(proj3, ext):
    b, s, _ = proj3.shape
    nq = s // TQ_A
    n_win = WIN_A // TQ_A

    def kv_spec(group, t):
        return pl.BlockSpec((None, TQ_A, WIDTH),
                            lambda bi, qi: (bi, jnp.maximum(qi - (n_win - 1) + t, 0), group))

    return pl.pallas_call(
        _attn_a_kernel,
        name="attn_a",
        grid=(b, nq),
        in_specs=[pl.BlockSpec((None, TQ_A, WIDTH), lambda bi, qi: (bi, qi, 0))]
        + [kv_spec(1, t) for t in range(n_win)]
        + [kv_spec(2, t) for t in range(n_win)]
        + [pl.BlockSpec(ext.shape, lambda bi, qi: (0, 0))],
        out_specs=pl.BlockSpec((None, TQ_A, WIDTH), lambda bi, qi: (bi, qi, 0)),
        out_shape=jax.ShapeDtypeStruct((b, s, WIDTH), BF16),
        scratch_shapes=[pltpu.VMEM((N_HEADS, TQ_A, WIN_A), F32)],
        compiler_params=pltpu.CompilerParams(
            dimension_semantics=("arbitrary", "arbitrary"), vmem_limit_bytes=VMEM_LIMIT),
    )(proj3, *([proj3] * (2 * n_win)), ext)


def _attn_b_kernel(q_ref, k_ref, v_ref, cq_ref, ck_ref, o_ref, m_sc, l_sc, acc_sc, *, hb):
    hp = pl.program_id(1)
    qi = pl.program_id(2)
    n_chunk = TK_B // LANES
    lane = lax.broadcasted_iota(jnp.int32, (TQ_B, LANES), 1)
    row = lax.broadcasted_iota(jnp.int32, (TQ_B, LANES), 0)
    qs, cqs = [], []
    for a in range(hb):
        qs.append(q_ref[:, a * HEAD_DIM:(a + 1) * HEAD_DIM])
        cq = jnp.sum(jnp.where(lane == hp * hb + a, cq_ref[...], 0.0), axis=-1, keepdims=True)
        cqs.append(jnp.broadcast_to(cq, (TQ_B, LANES)))
    m_sc[...] = jnp.full(m_sc.shape, NEG_INF, F32)
    l_sc[...] = jnp.zeros(l_sc.shape, F32)
    acc_sc[...] = jnp.zeros(acc_sc.shape, F32)

    def block(kb, masked):
        for a in range(hb):
            sl = slice(a * HEAD_DIM, (a + 1) * HEAD_DIM)
            k = k_ref[pl.ds(kb * TK_B, TK_B), sl]
            v = v_ref[pl.ds(kb * TK_B, TK_B), sl]
            s = _dot_nt(qs[a], k)
            ck = ck_ref[a, kb]
            chunks = []
            for c in range(n_chunk):
                cs = slice(c * LANES, (c + 1) * LANES)
                sc = s[:, cs] + (cqs[a] - ck[:, cs])
                if masked:
                    sc = jnp.where(lane + c * LANES <= row, sc, NEG_INF)
                chunks.append(sc)
            m_old = m_sc[a]
            m_new = jnp.maximum(m_old, jnp.max(functools.reduce(jnp.maximum, chunks), axis=-1, keepdims=True))
            alpha = jnp.exp(m_old - m_new)
            ps = [jnp.exp(sc - m_new) for sc in chunks]
            l_sc[a] = alpha * l_sc[a] + jnp.sum(functools.reduce(jnp.add, ps), axis=-1, keepdims=True)
            p = jnp.concatenate([pc.astype(BF16) for pc in ps], axis=1)
            acc_sc[a] = alpha * acc_sc[a] + _dot(p, v)
            m_sc[a] = m_new

    def body(kb, carry):
        block(kb, False)
        return carry

    lax.fori_loop(0, qi, body, 0)
    block(qi, True)
    for a in range(hb):
        o_ref[:, a * HEAD_DIM:(a + 1) * HEAD_DIM] = (acc_sc[a] / l_sc[a]).astype(BF16)


def _attn_b(proj3, cum3, cum_rows, hb=HB_B, name="attn_b"):
    b, s, _ = proj3.shape
    nq = s // TQ_B
    wb = hb * HEAD_DIM
    q0 = 3 * WIDTH // wb
    gstep = WIDTH // wb
    return pl.pallas_call(
        functools.partial(_attn_b_kernel, hb=hb),
        name=name,
        grid=(b, N_HEADS // hb, nq),
        in_specs=[
            pl.BlockSpec((None, TQ_B, wb), lambda bi, hp, qi: (bi, qi, q0 + hp)),
            pl.BlockSpec((None, s, wb), lambda bi, hp, qi: (bi, 0, q0 + gstep + hp)),
            pl.BlockSpec((None, s, wb), lambda bi, hp, qi: (bi, 0, q0 + 2 * gstep + hp)),
            pl.BlockSpec((None, TQ_B, LANES), lambda bi, hp, qi: (bi, qi, 0)),
            pl.BlockSpec((None, hb, s // TK_B, 1, TK_B), lambda bi, hp, qi: (bi, hp, 0, 0, 0)),
        ],
        out_specs=pl.BlockSpec((None, TQ_B, wb), lambda bi, hp, qi: (bi, qi, hp)),
        out_shape=jax.ShapeDtypeStruct((b, s, WIDTH), BF16),
        scratch_shapes=[pltpu.VMEM((hb, TQ_B, LANES), F32), pltpu.VMEM((hb, TQ_B, LANES), F32),
                        pltpu.VMEM((hb, TQ_B, HEAD_DIM), F32)],
        compiler_params=pltpu.CompilerParams(
            dimension_semantics=("arbitrary", "arbitrary", "arbitrary"), vmem_limit_bytes=VMEM_LIMIT),
    )(proj3, proj3, proj3, cum3, cum_rows)


def _merge_kernel(x_ref, ya_ref, yb_ref, ga_ref, gb_ref, wa_ref, wb_ref, wo_ref, fw_ref,
                  wr_hi_ref, wr_lo_ref, br_ref, tri_ref,
                  h_ref, hn_ref, route_ref, cnt_ref, carry_sc):
    i = pl.program_id(0)

    @pl.when(i == 0)
    def _():
        carry_sc[...] = jnp.zeros_like(carry_sc)

    za = _dot(ya_ref[...], wa_ref[...])
    zb = _dot(yb_ref[...], wb_ref[...])
    z = ga_ref[...].astype(F32) * za + gb_ref[...].astype(F32) * zb
    h = x_ref[...] + _dot(z.astype(BF16), wo_ref[...])
    h_ref[...] = h
    ms = jnp.mean(h * h, axis=-1, keepdims=True)
    hn = h * lax.rsqrt(ms + NORM_EPS) * fw_ref[...]
    half = hn.shape[1] // 2
    hn_ref[...] = _pack_bf16_pair(hn[:, :half], hn[:, half:])

    logits = _dot_f32x(hn, wr_hi_ref[...], wr_lo_ref[...]) + br_ref[...]
    tm = logits.shape[0]
    lane = lax.broadcasted_iota(jnp.int32, (tm, LANES), 1).astype(F32)
    work = logits
    vals, idxs = [], []
    for _ in range(TOP_K):
        m = jnp.max(work, axis=-1, keepdims=True)
        ix = jnp.min(jnp.where(work == m, lane, float(LANES)), axis=-1, keepdims=True)
        vals.append(m)
        idxs.append(ix)
        work = jnp.where(lane == ix, -jnp.inf, work)
    es = [jnp.exp(v - vals[0]) for v in vals]
    denom = functools.reduce(jnp.add, es)
    onehots = [(lane == ix).astype(F32) for ix in idxs]
    cnt = functools.reduce(jnp.add, onehots)
    before = _dot(tri_ref[...], cnt.astype(BF16)) + carry_sc[...]
    route = jnp.zeros((tm, LANES), F32)
    for k in range(TOP_K):
        rank = jnp.sum(onehots[k] * before, axis=-1, keepdims=True)
        route = jnp.where(lane == float(k), idxs[k], route)
        route = jnp.where(lane == float(TOP_K + k), es[k] / denom, route)
        route = jnp.where(lane == float(2 * TOP_K + k), rank, route)
    route_ref[...] = route
    total = carry_sc[...] + jnp.sum(cnt, axis=0, keepdims=True)
    carry_sc[...] = total
    cnt_ref[...] = total


def _merge(x2, ya, yb, proj, w_a, w_b, w_o, ffn_norm_w, wr_hi, wr_lo, br_pad):
    n, d = x2.shape
    tm = TM_MERGE
    tri = (jnp.arange(tm)[:, None] > jnp.arange(tm)[None, :]).astype(BF16)
    ga_blk = 3 * WIDTH * 2 // d
    const = lambda i: (0, 0)
    return pl.pallas_call(
        _merge_kernel,
        name="merge",
        grid=(n // tm,),
        in_specs=[
            pl.BlockSpec((tm, d), lambda i: (i, 0)),
            pl.BlockSpec((tm, WIDTH), lambda i: (i, 0)),
            pl.BlockSpec((tm, WIDTH), lambda i: (i, 0)),
            pl.BlockSpec((tm, d), lambda i: (i, ga_blk)),
            pl.BlockSpec((tm, d), lambda i: (i, ga_blk + 1)),
            pl.BlockSpec((WIDTH, d), const, pipeline_mode=pl.Buffered(1)),
            pl.BlockSpec((WIDTH, d), const, pipeline_mode=pl.Buffered(1)),
            pl.BlockSpec((d, d), const, pipeline_mode=pl.Buffered(1)),
            pl.BlockSpec((1, d), const),
            pl.BlockSpec((d, LANES), const),
            pl.BlockSpec((d, LANES), const),
            pl.BlockSpec((1, LANES), const),
            pl.BlockSpec((tm, tm), const),
        ],
        out_specs=[
            pl.BlockSpec((tm, d), lambda i: (i, 0)),
            pl.BlockSpec((tm, d // 2), lambda i: (i, 0)),
            pl.BlockSpec((tm, LANES), lambda i: (i, 0)),
            pl.BlockSpec((1, LANES), const),
        ],
        out_shape=[
            jax.ShapeDtypeStruct((n, d), F32),
            jax.ShapeDtypeStruct((n, d // 2), U32),
            jax.ShapeDtypeStruct((n, LANES), F32),
            jax.ShapeDtypeStruct((1, LANES), F32),
        ],
        scratch_shapes=[pltpu.VMEM((1, LANES), F32)],
        compiler_params=pltpu.CompilerParams(
            dimension_semantics=("arbitrary",), vmem_limit_bytes=VMEM_LIMIT),
    )(x2, ya, yb, proj, proj, w_a, w_b, w_o, ffn_norm_w.reshape(1, d), wr_hi, wr_lo, br_pad, tri)


def _dispatch_kernel(cnt_ref, dest_ref, hn_ref, xs_ref, zero_sc, sem, zsem, *, cap):
    i = pl.program_id(0)

    def row_copy(t, k):
        return pltpu.make_async_copy(hn_ref.at[pl.ds(t, 1)], xs_ref.at[pl.ds(dest_ref[t * TOP_K + k], 1)], sem)

    def issue(t, carry):
        for k in range(TOP_K):
            row_copy(t, k).start(priority=k % 2)
        return carry

    lax.fori_loop(0, TM_ROUTE, issue, 0, unroll=8)

    @pl.when(i == pl.num_programs(0) - 1)
    def _():
        zero_sc[...] = jnp.zeros(zero_sc.shape, zero_sc.dtype)

        def zero_fill(e, start):
            first = e * cap + cnt_ref[e]
            aligned = pl.multiple_of((first + SUBLANES - 1) // SUBLANES * SUBLANES, SUBLANES)
            for r in range(SUBLANES - 1):
                row = pltpu.make_async_copy(zero_sc.at[pl.ds(0, 1)], xs_ref.at[pl.ds(first + r, 1)], zsem)
                pl.when(first + r < aligned)(row.start if start else row.wait)
            block = pltpu.make_async_copy(zero_sc, xs_ref.at[pl.ds(aligned, TC_MOE)], zsem)
            block.start() if start else block.wait()

        for e in range(N_EXPERTS):
            zero_fill(e, True)
        for e in range(N_EXPERTS):
            zero_fill(e, False)

    def drain(t, carry):
        for k in range(TOP_K):
            row_copy(t, k).wait()
        return carry

    lax.fori_loop(0, TM_ROUTE, drain, 0, unroll=8)


def _dispatch(counts, dest_flat, hn, cap):
    n, d = hn.shape
    grid_spec = pltpu.PrefetchScalarGridSpec(
        num_scalar_prefetch=1,
        grid=(n // TM_ROUTE,),
        in_specs=[
            pl.BlockSpec((TM_ROUTE * TOP_K,), lambda i, cnt: (i,), memory_space=pltpu.SMEM),
            pl.BlockSpec((TM_ROUTE, d), lambda i, cnt: (i, 0)),
        ],
        out_specs=pl.BlockSpec(memory_space=pl.ANY),
        scratch_shapes=[pltpu.VMEM((TC_MOE, d), hn.dtype), pltpu.SemaphoreType.DMA, pltpu.SemaphoreType.DMA],
    )
    return pl.pallas_call(
        functools.partial(_dispatch_kernel, cap=cap),
        name="dispatch",
        grid_spec=grid_spec,
        out_shape=jax.ShapeDtypeStruct((N_EXPERTS * cap, d), hn.dtype),
        compiler_params=pltpu.CompilerParams(
            dimension_semantics=("arbitrary",), vmem_limit_bytes=VMEM_LIMIT),
    )(counts, dest_flat, hn)


def _moe_kernel(br_ref, be_ref, nc_ref, nb_ref, x_ref, wg_ref, wu_ref, wd_ref, bg_ref, bu_ref, bd_ref, o_ref,
                x_sc, acc_sc):
    i = pl.program_id(0)
    j = pl.program_id(1)
    d = acc_sc.shape[1]
    half = d // 2

    def process(off, m):
        rows = pl.ds(off, m)

        @pl.when(j == 0)
        def _():
            lo, hi = _unpack_bf16_pair(x_ref[rows, :])
            x_sc[rows, :half] = lo.astype(BF16)
            x_sc[rows, half:] = hi.astype(BF16)
            acc_sc[rows, :] = jnp.broadcast_to(bd_ref[...], (m, d))

        x = x_sc[rows, :]
        g = _dot(x, wg_ref[...].astype(BF16)) + bg_ref[...]
        u = _dot(x, wu_ref[...].astype(BF16)) + bu_ref[...]
        g = jnp.minimum(g, SWIGLU_LIMIT)
        u = jnp.clip(u, -SWIGLU_LIMIT, SWIGLU_LIMIT)
        hb = (u + 1.0) * (g * jax.nn.sigmoid(SWIGLU_ALPHA * g))
        acc_sc[rows, :] += _dot(hb.astype(BF16), wd_ref[...].astype(BF16))

        @pl.when(j == pl.num_programs(1) - 1)
        def _():
            y = acc_sc[rows, :]
            o_ref[rows, :] = _pack_bf16_pair(y[:, :half], y[:, half:])

    @pl.when(i < nb_ref[0])
    def _():
        n_chunks = nc_ref[i]
        odd = n_chunks % 2 == 1
        pl.when(n_chunks >= 4)(lambda: process(0, 4 * TC_MOE))
        pl.when(jnp.logical_or(n_chunks == 2, n_chunks == 3))(lambda: process(0, 2 * TC_MOE))
        pl.when(odd)(lambda: process(pl.multiple_of((n_chunks - 1) * TC_MOE, TC_MOE), TC_MOE))


def _moe(xs, blk_row, blk_expert, blk_chunks, n_blocks, w_gate_up, b_gate_up, w_down, b_down, max_blocks):
    rows, half = xs.shape
    d = 2 * half
    assert TM_MOE == 5 * TC_MOE
    nj = D_EXPERT // TF_MOE
    b_gu = b_gate_up.reshape(N_EXPERTS, 1, 2 * D_EXPERT)
    b_dn = b_down.reshape(N_EXPERTS, 1, d)

    def jj(i, j, nb):
        return jnp.where(i < nb[0], j, nj - 1)

    grid_spec = pltpu.PrefetchScalarGridSpec(
        num_scalar_prefetch=4,
        grid=(max_blocks, nj),
        in_specs=[
            pl.BlockSpec((TM_MOE, half), lambda i, j, br, be, nc, nb: (br[i], 0), pipeline_mode=pl.Buffered(1)),
            pl.BlockSpec((None, d, TF_MOE), lambda i, j, br, be, nc, nb: (be[i], 0, jj(i, j, nb))),
            pl.BlockSpec((None, d, TF_MOE), lambda i, j, br, be, nc, nb: (be[i], 0, jj(i, j, nb) + nj)),
            pl.BlockSpec((None, TF_MOE, d), lambda i, j, br, be, nc, nb: (be[i], jj(i, j, nb), 0)),
            pl.BlockSpec((None, 1, TF_MOE), lambda i, j, br, be, nc, nb: (be[i], 0, jj(i, j, nb))),
            pl.BlockSpec((None, 1, TF_MOE), lambda i, j, br, be, nc, nb: (be[i], 0, jj(i, j, nb) + nj)),
            pl.BlockSpec((None, 1, d), lambda i, j, br, be, nc, nb: (be[i], 0, 0)),
        ],
        out_specs=pl.BlockSpec((TM_MOE, half), lambda i, j, br, be, nc, nb: (br[i], 0)),
        scratch_shapes=[pltpu.VMEM((TM_MOE, d), BF16), pltpu.VMEM((TM_MOE, d), F32)],
    )
    return pl.pallas_call(
        _moe_kernel,
        name="moe",
        grid_spec=grid_spec,
        out_shape=jax.ShapeDtypeStruct((rows, half), U32),
        compiler_params=pltpu.CompilerParams(
            dimension_semantics=("arbitrary", "arbitrary"), vmem_limit_bytes=VMEM_LIMIT_MOE),
    )(blk_row, blk_expert, blk_chunks, n_blocks, xs, w_gate_up, w_gate_up, w_down, b_gu, b_gu, b_dn)


def _combine_kernel(dest_ref, dest_next_ref, h_ref, route_ref, ys_ref, o_ref, buf, sem):
    i = pl.program_id(0)
    slot = i % 2

    def row_copy(idx_ref, t, k, s):
        return pltpu.make_async_copy(ys_ref.at[pl.ds(idx_ref[t * TOP_K + k], 1)], buf.at[s, k, pl.ds(t, 1)],
                                     sem.at[s])

    def gather(idx_ref, s):
        def issue(t, carry):
            for k in range(TOP_K):
                row_copy(idx_ref, t, k, s).start(priority=k % 2)
            return carry
        lax.fori_loop(0, TM_ROUTE, issue, 0, unroll=8)

    pl.when(i == 0)(lambda: gather(dest_ref, slot))
    pl.when(i + 1 < pl.num_programs(0))(lambda: gather(dest_next_ref, 1 - slot))

    def drain(t, carry):
        for k in range(TOP_K):
            row_copy(dest_ref, t, k, slot).wait()
        return carry

    lax.fori_loop(0, TM_ROUTE, drain, 0, unroll=8)
    route = route_ref[...]
    half = buf.shape[3]
    acc_lo = h_ref[:, :half]
    acc_hi = h_ref[:, half:]
    for k in range(TOP_K):
        gate = route[:, TOP_K + k:TOP_K + k + 1]
        lo, hi = _unpack_bf16_pair(buf[slot, k])
        acc_lo = acc_lo + gate * lo
        acc_hi = acc_hi + gate * hi
    o_ref[:, :half] = acc_lo
    o_ref[:, half:] = acc_hi


def _combine(dest_flat, h, route, ys):
    n, d = h.shape
    n_tiles = n // TM_ROUTE
    return pl.pallas_call(
        _combine_kernel,
        name="combine",
        grid=(n_tiles,),
        in_specs=[
            pl.BlockSpec((TM_ROUTE * TOP_K,), lambda i: (i,), memory_space=pltpu.SMEM),
            pl.BlockSpec((TM_ROUTE * TOP_K,), lambda i: (jnp.minimum(i + 1, n_tiles - 1),),
                         memory_space=pltpu.SMEM),
            pl.BlockSpec((TM_ROUTE, d), lambda i: (i, 0)),
            pl.BlockSpec((TM_ROUTE, LANES), lambda i: (i, 0)),
            pl.BlockSpec(memory_space=pl.ANY),
        ],
        out_specs=pl.BlockSpec((TM_ROUTE, d), lambda i: (i, 0)),
        out_shape=jax.ShapeDtypeStruct((n, d), F32),
        scratch_shapes=[pltpu.VMEM((2, TOP_K, TM_ROUTE, d // 2), U32), pltpu.SemaphoreType.DMA((2,))],
        compiler_params=pltpu.CompilerParams(
            dimension_semantics=("arbitrary",), vmem_limit_bytes=VMEM_LIMIT),
    )(dest_flat, dest_flat, h, route, ys)


def _pad_lanes(a):
    return jnp.pad(a, ((0, 0), (0, LANES - a.shape[1])))


def kernel(x, attn_norm_w, w_in, b_forget, qn_a, kn_a, qn_b, kn_b, rel_bias, w_branch_a, w_branch_b,
           w_out, ffn_norm_w, w_router, b_router, w_gate_up, b_gate_up, w_down, b_down):
    b, s, d = x.shape
    n = b * s
    x2 = x.reshape(n, d)

    w_all = _w_prep(w_in)
    bf_pad = _pad_lanes(b_forget.reshape(1, N_HEADS))
    scale = HEAD_DIM ** -0.5
    ones = jnp.ones((HEAD_DIM,), F32)
    head_norm_w = jnp.stack([qn_a * scale, kn_a, ones, qn_b * scale, kn_b, ones, ones, ones])
    wr = _pad_lanes(w_router)
    wr_hi = wr.astype(BF16)
    wr_lo = (wr - wr_hi.astype(F32)).astype(BF16)
    br_pad = jnp.concatenate([b_router.reshape(1, N_EXPERTS),
                              jnp.full((1, LANES - N_EXPERTS), NEG_INF, F32)], axis=1)

    proj, cum = _in_proj(x2, attn_norm_w, w_all, w_in, bf_pad, head_norm_w, s)
    proj3 = proj.reshape(b, s, proj.shape[1])

    y_a = _attn_a(proj3, _attn_a_ext_table(rel_bias))
    cum3 = cum.reshape(b, s, LANES)
    cum_rows = jnp.transpose(cum3[:, :, :N_HEADS], (0, 2, 1)).reshape(b, N_HEADS, s // TK_B, 1, TK_B)
    y_b = _attn_b(proj3, cum3, cum_rows)

    h, hn, route, cnt = _merge(x2, y_a.reshape(n, WIDTH), y_b.reshape(n, WIDTH), proj,
                                w_branch_a.astype(BF16), w_branch_b.astype(BF16), w_out.astype(BF16),
                                ffn_norm_w, wr_hi, wr_lo, br_pad)

    blocks_per_expert = -(-n // TM_MOE) + 1
    cap = blocks_per_expert * TM_MOE
    top_e = route[:, :TOP_K].astype(jnp.int32)
    rank = route[:, 2 * TOP_K:3 * TOP_K].astype(jnp.int32)
    dest_flat = (top_e * cap + rank).reshape(n * TOP_K)
    counts = cnt[0, :N_EXPERTS].astype(jnp.int32)
    nblk = (counts + TM_MOE - 1) // TM_MOE
    bend = jnp.cumsum(nblk)
    bstart = bend - nblk
    max_blocks = -(-n * TOP_K // TM_MOE) + N_EXPERTS
    blk = jnp.minimum(jnp.arange(max_blocks), bend[-1] - 1)
    blk_expert = jnp.minimum(jnp.sum(blk[:, None] >= bend[None, :], axis=1), N_EXPERTS - 1).astype(jnp.int32)
    blk_in_expert = blk - bstart[blk_expert]
    blk_row = (blk_expert * blocks_per_expert + blk_in_expert).astype(jnp.int32)
    blk_tokens = jnp.clip(counts[blk_expert] - blk_in_expert * TM_MOE, 0, TM_MOE)
    blk_chunks = ((blk_tokens + TC_MOE - 1) // TC_MOE).astype(jnp.int32)
    n_blocks = bend[-1].astype(jnp.int32).reshape(1)

    xs = _dispatch(counts, dest_flat, hn, cap)
    ys = _moe(xs, blk_row, blk_expert, blk_chunks, n_blocks, w_gate_up, b_gate_up, w_down, b_down, max_blocks)
    out = _combine(dest_flat, h, route, ys)
    return out.reshape(b, s, d)
```

```python
import functools

import jax
import jax.numpy as jnp
from jax import lax
from jax.experimental import pallas as pl
from jax.experimental.pallas import tpu as pltpu

D_MODEL = 2048
CHUNK = 64
LEFT_CHUNKS = 8
BAND = (LEFT_CHUNKS + 1) * CHUNK
HEAD_DIM = 128
N_HEADS = 8
WIDTH = N_HEADS * HEAD_DIM
REL_CLIP = 256
N_EXPERTS = 32
TOP_K = 4
D_EXPERT = D_MODEL
SWIGLU_LIMIT = 7.0
SWIGLU_ALPHA = 1.702
NORM_EPS = 1e-5
NEG_INF = -1e30

LANES = 128
SUBLANES = 8
VMEM_LIMIT = 56 * 1024 * 1024
VMEM_LIMIT_MOE = 60 * 1024 * 1024

TM_PROJ = 1024
TN_PROJ = 1024
TQ_A = 256
WIN_A = TQ_A + LEFT_CHUNKS * CHUNK
TQ_B = 512
TK_B = 512
HB_B = 4
TM_MERGE = 256
TM_ROUTE = 256
TM_MOE = 1024
TC_MOE = 256
TF_MOE = 512

F32 = jnp.float32
BF16 = jnp.bfloat16
U32 = jnp.uint32


def _dot(a, b):
    return jnp.dot(a, b, preferred_element_type=F32)


def _dot_nt(a, b):
    return lax.dot_general(a, b, (((1,), (1,)), ((), ())), preferred_element_type=F32)


def _pack_bf16_pair(lo, hi):
    lo_bits = lax.bitcast_convert_type(lo.astype(BF16).astype(F32), U32)
    hi_bits = lax.bitcast_convert_type(hi.astype(BF16).astype(F32), U32)
    return (lo_bits >> 16) | hi_bits


def _unpack_bf16_pair(w):
    lo = lax.bitcast_convert_type(w << 16, F32)
    hi = lax.bitcast_convert_type(w & jnp.uint32(0xFFFF0000), F32)
    return lo, hi


def _split3(x):
    hi = x.astype(BF16)
    r = x - hi.astype(F32)
    mid = r.astype(BF16)
    lo = (r - mid.astype(F32)).astype(BF16)
    return hi, mid, lo


def _dot_f32x(x, w_hi, w_lo):
    hi, mid, _ = _split3(x)
    return _dot(hi, w_hi) + (_dot(hi, w_lo) + _dot(mid, w_hi))


def _w_prep_kernel(a_ref, b_ref, o_ref, *, n_plain):
    j = pl.program_id(0)

    @pl.when(j < n_plain)
    def _():
        o_ref[...] = a_ref[...].astype(BF16)

    @pl.when(j >= n_plain)
    def _():
        tn = a_ref.shape[1]
        ra = pltpu.roll(a_ref[...], tn - N_HEADS, 1)
        rb = pltpu.roll(b_ref[...], LANES - N_HEADS, 1)
        lane = lax.broadcasted_iota(jnp.int32, rb.shape, 1)
        tail = jnp.where(lane < LANES - N_HEADS, ra[:, tn - LANES:], rb)
        o_ref[:, :tn - LANES] = ra[:, :tn - LANES].astype(BF16)
        o_ref[:, tn - LANES:] = tail.astype(BF16)


def _w_prep(w_in):
    d, cols = w_in.shape
    n_out = (cols - N_HEADS) // TN_PROJ
    n_plain = 6 * WIDTH // TN_PROJ
    return pl.pallas_call(
        functools.partial(_w_prep_kernel, n_plain=n_plain),
        name="w_prep",
        grid=(n_out,),
        in_specs=[
            pl.BlockSpec((d, TN_PROJ), lambda j: (0, j)),
            pl.BlockSpec((d, LANES), lambda j: (0, (j + 1) * (TN_PROJ // LANES))),
        ],
        out_specs=pl.BlockSpec((d, TN_PROJ), lambda j: (0, j)),
        out_shape=jax.ShapeDtypeStruct((d, n_out * TN_PROJ), BF16),
        compiler_params=pltpu.CompilerParams(
            dimension_semantics=("arbitrary",), vmem_limit_bytes=VMEM_LIMIT),
    )(w_in, w_in)


def _in_proj_kernel(x_ref, nw_ref, w_ref, wf_ref, bf_ref, hn_ref, tri_ref,
                    out_ref, cum_ref, xn_sc, carry_sc, *, tiles_per_seq):
    i = pl.program_id(0)
    j = pl.program_id(1)

    @pl.when(j == 0)
    def _():
        x = x_ref[...]
        ms = jnp.mean(x * x, axis=-1, keepdims=True)
        xn = x * lax.rsqrt(ms + NORM_EPS) * nw_ref[...]
        xn_sc[...] = xn.astype(BF16)
        lane = lax.broadcasted_iota(jnp.int32, wf_ref.shape, 1)
        wf = jnp.where(lane < N_HEADS, wf_ref[...], 0.0)
        wf_hi = wf.astype(BF16)
        wf_lo = (wf - wf_hi.astype(F32)).astype(BF16)
        z = _dot_f32x(xn, wf_hi, wf_lo) + bf_ref[...]
        logf = jnp.minimum(z, 0.0) - jnp.log1p(jnp.exp(-jnp.abs(z)))
        hi, mid, lo = _split3(logf)
        tri = tri_ref[...]
        c = _dot(tri, hi) + (_dot(tri, mid) + _dot(tri, lo))
        carry = jnp.where(i % tiles_per_seq == 0, 0.0, carry_sc[...])
        c = c + carry
        cum_ref[...] = c
        carry_sc[...] = c[-1:, :]

    acc = _dot(xn_sc[...], w_ref[...])
    is_norm = jnp.logical_and(j < 6, jnp.logical_and(j != 2, j != 5))

    @pl.when(is_norm)
    def _():
        w = hn_ref[pl.ds(j, 1), :]
        for h in range(N_HEADS):
            sl = slice(h * HEAD_DIM, (h + 1) * HEAD_DIM)
            t = acc[:, sl]
            ms = jnp.mean(t * t, axis=-1, keepdims=True)
            out_ref[:, sl] = (t * lax.rsqrt(ms + NORM_EPS) * w).astype(BF16)

    @pl.when(jnp.logical_or(j == 2, j == 5))
    def _():
        out_ref[...] = acc.astype(BF16)

    @pl.when(j >= 6)
    def _():
        out_ref[...] = jax.nn.sigmoid(acc).astype(BF16)


def _in_proj(x2, attn_norm_w, w_all, w_in, bf_pad, head_norm_w, seq, tm=TM_PROJ, name="in_proj"):
    n, d = x2.shape
    n_col = w_all.shape[1] // TN_PROJ
    tri = (jnp.arange(tm)[:, None] >= jnp.arange(tm)[None, :]).astype(BF16)
    return pl.pallas_call(
        functools.partial(_in_proj_kernel, tiles_per_seq=seq // tm),
        name=name,
        grid=(n // tm, n_col),
        in_specs=[
            pl.BlockSpec((tm, d), lambda i, j: (i, 0)),
            pl.BlockSpec((1, d), lambda i, j: (0, 0)),
            pl.BlockSpec((d, TN_PROJ), lambda i, j: (0, j)),
            pl.BlockSpec((d, LANES), lambda i, j: (0, 6 * WIDTH // LANES)),
            pl.BlockSpec((1, LANES), lambda i, j: (0, 0)),
            pl.BlockSpec(head_norm_w.shape, lambda i, j: (0, 0)),
            pl.BlockSpec((tm, tm), lambda i, j: (0, 0)),
        ],
        out_specs=[
            pl.BlockSpec((tm, TN_PROJ), lambda i, j: (i, j)),
            pl.BlockSpec((tm, LANES), lambda i, j: (i, 0)),
        ],
        out_shape=[
            jax.ShapeDtypeStruct((n, w_all.shape[1]), BF16),
            jax.ShapeDtypeStruct((n, LANES), F32),
        ],
        scratch_shapes=[pltpu.VMEM((tm, d), BF16), pltpu.VMEM((1, LANES), F32)],
        compiler_params=pltpu.CompilerParams(
            dimension_semantics=("arbitrary", "arbitrary"), vmem_limit_bytes=VMEM_LIMIT),
    )(x2, attn_norm_w.reshape(1, d), w_all, w_in, bf_pad, head_norm_w, tri)


def _attn_a_kernel(q_ref, k0_ref, k1_ref, k2_ref, v0_ref, v1_ref, v2_ref, ext_ref, o_ref, bias_ref):
    qb = pl.program_id(1)

    @pl.when(jnp.logical_and(pl.program_id(0) == 0, qb == 0))
    def _():
        ext_len = ext_ref.shape[1]
        r = lax.broadcasted_iota(jnp.int32, (TQ_A, WIN_A), 0)
        c = lax.broadcasted_iota(jnp.int32, (TQ_A, WIN_A), 1)
        lo = (r // CHUNK) * CHUNK
        in_band = jnp.logical_and(c >= lo, c < lo + BAND)
        for h in range(N_HEADS):
            rows = jnp.broadcast_to(ext_ref[pl.ds(h, 1), :], (TQ_A, ext_len))
            win = pltpu.roll(rows, ext_len - (TQ_A - 1), 1, stride=1, stride_axis=0)[:, :WIN_A]
            bias_ref[h] = jnp.where(in_band, win, NEG_INF)
    k_refs = (k0_ref, k1_ref, k2_ref)
    v_refs = (v0_ref, v1_ref, v2_ref)
    n_win = len(k_refs)
    for h in range(N_HEADS):
        sl = slice(h * HEAD_DIM, (h + 1) * HEAD_DIM)
        q = q_ref[:, sl]
        scores = []
        for t in range(n_win):
            s = _dot_nt(q, k_refs[t][:, sl]) + bias_ref[h, :, t * TQ_A:(t + 1) * TQ_A]
            if t < n_win - 1:
                s = jnp.where(qb >= n_win - 1 - t, s, NEG_INF)
            scores.append(s)
        m = functools.reduce(jnp.maximum, [jnp.max(s, axis=-1, keepdims=True) for s in scores])
        ps = [jnp.exp(s - m) for s in scores]
        l = functools.reduce(jnp.add, [jnp.sum(p, axis=-1, keepdims=True) for p in ps])
        o = functools.reduce(jnp.add, [_dot(ps[t].astype(BF16), v_refs[t][:, sl]) for t in range(n_win)])
        o_ref[:, sl] = (o / l).astype(BF16)


def _attn_a_ext_table(rel_bias):
    j = jnp.arange(TQ_A + WIN_A)
    rel = (TQ_A - 1 + LEFT_CHUNKS * CHUNK) - j
    return rel_bias[:, jnp.clip(rel, -(CHUNK - 1), REL_CLIP) + (CHUNK - 1)].astype(F32)


def _attn_a(proj3, ext):
    b, s, _ = proj3.shape
    nq = s // TQ_A
    n_win = WIN_A // TQ_A

    def kv_spec(group, t):
        return pl.BlockSpec((None, TQ_A, WIDTH),
                            lambda bi, qi: (bi, jnp.maximum(qi - (n_win - 1) + t, 0), group))

    return pl.pallas_call(
        _attn_a_kernel,
        name="attn_a",
        grid=(b, nq),
        in_specs=[pl.BlockSpec((None, TQ_A, WIDTH), lambda bi, qi: (bi, qi, 0))]
        + [kv_spec(1, t) for t in range(n_win)]
        + [kv_spec(2, t) for t in range(n_win)]
        + [pl.BlockSpec(ext.shape, lambda bi, qi: (0, 0))],
        out_specs=pl.BlockSpec((None, TQ_A, WIDTH), lambda bi, qi: (bi, qi, 0)),
        out_shape=jax.ShapeDtypeStruct((b, s, WIDTH), BF16),
        scratch_shapes=[pltpu.VMEM((N_HEADS, TQ_A, WIN_A), F32)],
        compiler_params=pltpu.CompilerParams(
            dimension_semantics=("arbitrary", "arbitrary"), vmem_limit_bytes=VMEM_LIMIT),
    )(proj3, *([proj3] * (2 * n_win)), ext)


def _attn_b_kernel(q_ref, k_ref, v_ref, cq_ref, ck_ref, o_ref, m_sc, l_sc, acc_sc, *, hb):
    hp = pl.program_id(1)
    qi = pl.program_id(2)
    n_chunk = TK_B // LANES
    lane = lax.broadcasted_iota(jnp.int32, (TQ_B, LANES), 1)
    row = lax.broadcasted_iota(jnp.int32, (TQ_B, LANES), 0)
    qs, cqs = [], []
    for a in range(hb):
        qs.append(q_ref[:, a * HEAD_DIM:(a + 1) * HEAD_DIM])
        cq = jnp.sum(jnp.where(lane == hp * hb + a, cq_ref[...], 0.0), axis=-1, keepdims=True)
        cqs.append(jnp.broadcast_to(cq, (TQ_B, LANES)))
    m_sc[...] = jnp.full(m_sc.shape, NEG_INF, F32)
    l_sc[...] = jnp.zeros(l_sc.shape, F32)
    acc_sc[...] = jnp.zeros(acc_sc.shape, F32)

    def block(kb, masked):
        for a in range(hb):
            sl = slice(a * HEAD_DIM, (a + 1) * HEAD_DIM)
            k = k_ref[pl.ds(kb * TK_B, TK_B), sl]
            v = v_ref[pl.ds(kb * TK_B, TK_B), sl]
            s = _dot_nt(qs[a], k)
            ck = ck_ref[a, kb]
            chunks = []
            for c in range(n_chunk):
                cs = slice(c * LANES, (c + 1) * LANES)
                sc = s[:, cs] + (cqs[a] - ck[:, cs])
                if masked:
                    sc = jnp.where(lane + c * LANES <= row, sc, NEG_INF)
                chunks.append(sc)
            m_old = m_sc[a]
            m_new = jnp.maximum(m_old, jnp.max(functools.reduce(jnp.maximum, chunks), axis=-1, keepdims=True))
            alpha = jnp.exp(m_old - m_new)
            ps = [jnp.exp(sc - m_new) for sc in chunks]
            l_sc[a] = alpha * l_sc[a] + jnp.sum(functools.reduce(jnp.add, ps), axis=-1, keepdims=True)
            p = jnp.concatenate([pc.astype(BF16) for pc in ps], axis=1)
            acc_sc[a] = alpha * acc_sc[a] + _dot(p, v)
            m_sc[a] = m_new

    def body(kb, carry):
        block(kb, False)
        return carry

    lax.fori_loop(0, qi, body, 0)
    block(qi, True)
    for a in range(hb):
        o_ref[:, a * HEAD_DIM:(a + 1) * HEAD_DIM] = (acc_sc[a] / l_sc[a]).astype(BF16)


def _attn_b(proj3, cum3, cum_rows, hb=HB_B, name="attn_b"):
    b, s, _ = proj3.shape
    nq = s // TQ_B
    wb = hb * HEAD_DIM
    q0 = 3 * WIDTH // wb
    gstep = WIDTH // wb
    return pl.pallas_call(
        functools.partial(_attn_b_kernel, hb=hb),
        name=name,
        grid=(b, N_HEADS // hb, nq),
        in_specs=[
            pl.BlockSpec((None, TQ_B, wb), lambda bi, hp, qi: (bi, qi, q0 + hp)),
            pl.BlockSpec((None, s, wb), lambda bi, hp, qi: (bi, 0, q0 + gstep + hp)),
            pl.BlockSpec((None, s, wb), lambda bi, hp, qi: (bi, 0, q0 + 2 * gstep + hp)),
            pl.BlockSpec((None, TQ_B, LANES), lambda bi, hp, qi: (bi, qi, 0)),
            pl.BlockSpec((None, hb, s // TK_B, 1, TK_B), lambda bi, hp, qi: (bi, hp, 0, 0, 0)),
        ],
        out_specs=pl.BlockSpec((None, TQ_B, wb), lambda bi, hp, qi: (bi, qi, hp)),
        out_shape=jax.ShapeDtypeStruct((b, s, WIDTH), BF16),
        scratch_shapes=[pltpu.VMEM((hb, TQ_B, LANES), F32), pltpu.VMEM((hb, TQ_B, LANES), F32),
                        pltpu.VMEM((hb, TQ_B, HEAD_DIM), F32)],
        compiler_params=pltpu.CompilerParams(
            dimension_semantics=("arbitrary", "arbitrary", "arbitrary"), vmem_limit_bytes=VMEM_LIMIT),
    )(proj3, proj3, proj3, cum3, cum_rows)


def _merge_kernel(x_ref, ya_ref, yb_ref, ga_ref, gb_ref, wa_ref, wb_ref, wo_ref, fw_ref,
                  wr_hi_ref, wr_lo_ref, br_ref, tri_ref,
                  h_ref, hn_ref, route_ref, cnt_ref, carry_sc):
    i = pl.program_id(0)

    @pl.when(i == 0)
    def _():
        carry_sc[...] = jnp.zeros_like(carry_sc)

    za = _dot(ya_ref[...], wa_ref[...])
    zb = _dot(yb_ref[...], wb_ref[...])
    z = ga_ref[...].astype(F32) * za + gb_ref[...].astype(F32) * zb
    h = x_ref[...] + _dot(z.astype(BF16), wo_ref[...])
    h_ref[...] = h
    ms = jnp.mean(h * h, axis=-1, keepdims=True)
    hn = h * lax.rsqrt(ms + NORM_EPS) * fw_ref[...]
    half = hn.shape[1] // 2
    hn_ref[...] = _pack_bf16_pair(hn[:, :half], hn[:, half:])

    logits = _dot_f32x(hn, wr_hi_ref[...], wr_lo_ref[...]) + br_ref[...]
    tm = logits.shape[0]
    lane = lax.broadcasted_iota(jnp.int32, (tm, LANES), 1).astype(F32)
    work = logits
    vals, idxs = [], []
    for _ in range(TOP_K):
        m = jnp.max(work, axis=-1, keepdims=True)
        ix = jnp.min(jnp.where(work == m, lane, float(LANES)), axis=-1, keepdims=True)
        vals.append(m)
        idxs.append(ix)
        work = jnp.where(lane == ix, -jnp.inf, work)
    es = [jnp.exp(v - vals[0]) for v in vals]
    denom = functools.reduce(jnp.add, es)
    onehots = [(lane == ix).astype(F32) for ix in idxs]
    cnt = functools.reduce(jnp.add, onehots)
    before = _dot(tri_ref[...], cnt.astype(BF16)) + carry_sc[...]
    route = jnp.zeros((tm, LANES), F32)
    for k in range(TOP_K):
        rank = jnp.sum(onehots[k] * before, axis=-1, keepdims=True)
        route = jnp.where(lane == float(k), idxs[k], route)
        route = jnp.where(lane == float(TOP_K + k), es[k] / denom, route)
        route = jnp.where(lane == float(2 * TOP_K + k), rank, route)
    route_ref[...] = route
    total = carry_sc[...] + jnp.sum(cnt, axis=0, keepdims=True)
    carry_sc[...] = total
    cnt_ref[...] = total


def _merge(x2, ya, yb, proj, w_a, w_b, w_o, ffn_norm_w, wr_hi, wr_lo, br_pad):
    n, d = x2.shape
    tm = TM_MERGE
    tri = (jnp.arange(tm)[:, None] > jnp.arange(tm)[None, :]).astype(BF16)
    ga_blk = 3 * WIDTH * 2 // d
    const = lambda i: (0, 0)
    return pl.pallas_call(
        _merge_kernel,
        name="merge",
        grid=(n // tm,),
        in_specs=[
            pl.BlockSpec((tm, d), lambda i: (i, 0)),
            pl.BlockSpec((tm, WIDTH), lambda i: (i, 0)),
            pl.BlockSpec((tm, WIDTH), lambda i: (i, 0)),
            pl.BlockSpec((tm, d), lambda i: (i, ga_blk)),
            pl.BlockSpec((tm, d), lambda i: (i, ga_blk + 1)),
            pl.BlockSpec((WIDTH, d), const, pipeline_mode=pl.Buffered(1)),
            pl.BlockSpec((WIDTH, d), const, pipeline_mode=pl.Buffered(1)),
            pl.BlockSpec((d, d), const, pipeline_mode=pl.Buffered(1)),
            pl.BlockSpec((1, d), const),
            pl.BlockSpec((d, LANES), const),
            pl.BlockSpec((d, LANES), const),
            pl.BlockSpec((1, LANES), const),
            pl.BlockSpec((tm, tm), const),
        ],
        out_specs=[
            pl.BlockSpec((tm, d), lambda i: (i, 0)),
            pl.BlockSpec((tm, d // 2), lambda i: (i, 0)),
            pl.BlockSpec((tm, LANES), lambda i: (i, 0)),
            pl.BlockSpec((1, LANES), const),
        ],
        out_shape=[
            jax.ShapeDtypeStruct((n, d), F32),
            jax.ShapeDtypeStruct((n, d // 2), U32),
            jax.ShapeDtypeStruct((n, LANES), F32),
            jax.ShapeDtypeStruct((1, LANES), F32),
        ],
        scratch_shapes=[pltpu.VMEM((1, LANES), F32)],
        compiler_params=pltpu.CompilerParams(
            dimension_semantics=("arbitrary",), vmem_limit_bytes=VMEM_LIMIT),
    )(x2, ya, yb, proj, proj, w_a, w_b, w_o, ffn_norm_w.reshape(1, d), wr_hi, wr_lo, br_pad, tri)


def _dispatch_kernel(cnt_ref, dest_ref, hn_ref, xs_ref, zero_sc, sem, zsem, *, cap):
    i = pl.program_id(0)

    def row_copy(t, k):
        return pltpu.make_async_copy(hn_ref.at[pl.ds(t, 1)], xs_ref.at[pl.ds(dest_ref[t * TOP_K + k], 1)], sem)

    def issue(t, carry):
        for k in range(TOP_K):
            row_copy(t, k).start(priority=k % 2)
        return carry

    lax.fori_loop(0, TM_ROUTE, issue, 0, unroll=8)

    @pl.when(i == pl.num_programs(0) - 1)
    def _():
        zero_sc[...] = jnp.zeros(zero_sc.shape, zero_sc.dtype)

        def zero_fill(e, start):
            first = e * cap + cnt_ref[e]
            aligned = pl.multiple_of((first + SUBLANES - 1) // SUBLANES * SUBLANES, SUBLANES)
            for r in range(SUBLANES - 1):
                row = pltpu.make_async_copy(zero_sc.at[pl.ds(0, 1)], xs_ref.at[pl.ds(first + r, 1)], zsem)
                pl.when(first + r < aligned)(row.start if start else row.wait)
            block = pltpu.make_async_copy(zero_sc, xs_ref.at[pl.ds(aligned, TC_MOE)], zsem)
            block.start() if start else block.wait()

        for e in range(N_EXPERTS):
            zero_fill(e, True)
        for e in range(N_EXPERTS):
            zero_fill(e, False)

    def drain(t, carry):
        for k in range(TOP_K):
            row_copy(t, k).wait()
        return carry

    lax.fori_loop(0, TM_ROUTE, drain, 0, unroll=8)


def _dispatch(counts, dest_flat, hn, cap):
    n, d = hn.shape
    grid_spec = pltpu.PrefetchScalarGridSpec(
        num_scalar_prefetch=1,
        grid=(n // TM_ROUTE,),
        in_specs=[
            pl.BlockSpec((TM_ROUTE * TOP_K,), lambda i, cnt: (i,), memory_space=pltpu.SMEM),
            pl.BlockSpec((TM_ROUTE, d), lambda i, cnt: (i, 0)),
        ],
        out_specs=pl.BlockSpec(memory_space=pl.ANY),
        scratch_shapes=[pltpu.VMEM((TC_MOE, d), hn.dtype), pltpu.SemaphoreType.DMA, pltpu.SemaphoreType.DMA],
    )
    return pl.pallas_call(
        functools.partial(_dispatch_kernel, cap=cap),
        name="dispatch",
        grid_spec=grid_spec,
        out_shape=jax.ShapeDtypeStruct((N_EXPERTS * cap, d), hn.dtype),
        compiler_params=pltpu.CompilerParams(
            dimension_semantics=("arbitrary",), vmem_limit_bytes=VMEM_LIMIT),
    )(counts, dest_flat, hn)


def _moe_kernel(br_ref, be_ref, nc_ref, nb_ref, x_ref, wg_ref, wu_ref, wd_ref, bg_ref, bu_ref, bd_ref, o_ref,
                x_sc, acc_sc):
    i = pl.program_id(0)
    j = pl.program_id(1)
    d = acc_sc.shape[1]
    half = d // 2

    def process(off, m):
        rows = pl.ds(off, m)

        @pl.when(j == 0)
        def _():
            lo, hi = _unpack_bf16_pair(x_ref[rows, :])
            x_sc[rows, :half] = lo.astype(BF16)
            x_sc[rows, half:] = hi.astype(BF16)
            acc_sc[rows, :] = jnp.broadcast_to(bd_ref[...], (m, d))

        x = x_sc[rows, :]
        g = _dot(x, wg_ref[...].astype(BF16)) + bg_ref[...]
        u = _dot(x, wu_ref[...].astype(BF16)) + bu_ref[...]
        g = jnp.minimum(g, SWIGLU_LIMIT)
        u = jnp.clip(u, -SWIGLU_LIMIT, SWIGLU_LIMIT)
        hb = (u + 1.0) * (g * jax.nn.sigmoid(SWIGLU_ALPHA * g))
        acc_sc[rows, :] += _dot(hb.astype(BF16), wd_ref[...].astype(BF16))

        @pl.when(j == pl.num_programs(1) - 1)
        def _():
            y = acc_sc[rows, :]
            o_ref[rows, :] = _pack_bf16_pair(y[:, :half], y[:, half:])

    @pl.when(i < nb_ref[0])
    def _():
        n_chunks = nc_ref[i]
        pl.when(n_chunks == 4)(lambda: process(0, 4 * TC_MOE))
        pl.when(jnp.logical_or(n_chunks == 2, n_chunks == 3))(lambda: process(0, 2 * TC_MOE))
        pl.when(jnp.logical_or(n_chunks == 1, n_chunks == 3))(
            lambda: process(pl.multiple_of((n_chunks - 1) * TC_MOE, TC_MOE), TC_MOE))


def _moe(xs, blk_row, blk_expert, blk_chunks, n_blocks, w_gate_up, b_gate_up, w_down, b_down, max_blocks):
    rows, half = xs.shape
    d = 2 * half
    assert TM_MOE == 4 * TC_MOE
    nj = D_EXPERT // TF_MOE
    b_gu = b_gate_up.reshape(N_EXPERTS, 1, 2 * D_EXPERT)
    b_dn = b_down.reshape(N_EXPERTS, 1, d)

    def jj(i, j, nb):
        snake = lambda blk, step: jnp.where(blk % 2 == 1, nj - 1 - step, step)
        return jnp.where(i < nb[0], snake(i, j), snake(nb[0] - 1, nj - 1))

    grid_spec = pltpu.PrefetchScalarGridSpec(
        num_scalar_prefetch=4,
        grid=(max_blocks, nj),
        in_specs=[
            pl.BlockSpec((TM_MOE, half), lambda i, j, br, be, nc, nb: (br[i], 0)),
            pl.BlockSpec((None, d, TF_MOE), lambda i, j, br, be, nc, nb: (be[i], 0, jj(i, j, nb))),
            pl.BlockSpec((None, d, TF_MOE), lambda i, j, br, be, nc, nb: (be[i], 0, jj(i, j, nb) + nj)),
            pl.BlockSpec((None, TF_MOE, d), lambda i, j, br, be, nc, nb: (be[i], jj(i, j, nb), 0)),
            pl.BlockSpec((None, 1, TF_MOE), lambda i, j, br, be, nc, nb: (be[i], 0, jj(i, j, nb))),
            pl.BlockSpec((None, 1, TF_MOE), lambda i, j, br, be, nc, nb: (be[i], 0, jj(i, j, nb) + nj)),
            pl.BlockSpec((None, 1, d), lambda i, j, br, be, nc, nb: (be[i], 0, 0)),
        ],
        out_specs=pl.BlockSpec((TM_MOE, half), lambda i, j, br, be, nc, nb: (br[i], 0)),
        scratch_shapes=[pltpu.VMEM((TM_MOE, d), BF16), pltpu.VMEM((TM_MOE, d), F32)],
    )
    return pl.pallas_call(
        _moe_kernel,
        name="moe",
        grid_spec=grid_spec,
        out_shape=jax.ShapeDtypeStruct((rows, half), U32),
        compiler_params=pltpu.CompilerParams(
            dimension_semantics=("arbitrary", "arbitrary"), vmem_limit_bytes=VMEM_LIMIT_MOE),
    )(blk_row, blk_expert, blk_chunks, n_blocks, xs, w_gate_up, w_gate_up, w_down, b_gu, b_gu, b_dn)


def _combine_kernel(dest_ref, dest_next_ref, h_ref, route_ref, ys_ref, o_ref, buf, sem):
    i = pl.program_id(0)
    slot = i % 2

    def row_copy(idx_ref, t, k, s):
        return pltpu.make_async_copy(ys_ref.at[pl.ds(idx_ref[t * TOP_K + k], 1)], buf.at[s, k, pl.ds(t, 1)],
                                     sem.at[s])

    def gather(idx_ref, s):
        def issue(t, carry):
            for k in range(TOP_K):
                row_copy(idx_ref, t, k, s).start(priority=k % 2)
            return carry
        lax.fori_loop(0, TM_ROUTE, issue, 0, unroll=8)

    pl.when(i == 0)(lambda: gather(dest_ref, slot))
    pl.when(i + 1 < pl.num_programs(0))(lambda: gather(dest_next_ref, 1 - slot))

    def drain(t, carry):
        for k in range(TOP_K):
            row_copy(dest_ref, t, k, slot).wait()
        return carry

    lax.fori_loop(0, TM_ROUTE, drain, 0, unroll=8)
    route = route_ref[...]
    half = buf.shape[3]
    acc_lo = h_ref[:, :half]
    acc_hi = h_ref[:, half:]
    for k in range(TOP_K):
        gate = route[:, TOP_K + k:TOP_K + k + 1]
        lo, hi = _unpack_bf16_pair(buf[slot, k])
        acc_lo = acc_lo + gate * lo
        acc_hi = acc_hi + gate * hi
    o_ref[:, :half] = acc_lo
    o_ref[:, half:] = acc_hi


def _combine(dest_flat, h, route, ys):
    n, d = h.shape
    n_tiles = n // TM_ROUTE
    return pl.pallas_call(
        _combine_kernel,
        name="combine",
        grid=(n_tiles,),
        in_specs=[
            pl.BlockSpec((TM_ROUTE * TOP_K,), lambda i: (i,), memory_space=pltpu.SMEM),
            pl.BlockSpec((TM_ROUTE * TOP_K,), lambda i: (jnp.minimum(i + 1, n_tiles - 1),),
                         memory_space=pltpu.SMEM),
            pl.BlockSpec((TM_ROUTE, d), lambda i: (i, 0)),
            pl.BlockSpec((TM_ROUTE, LANES), lambda i: (i, 0)),
            pl.BlockSpec(memory_space=pl.ANY),
        ],
        out_specs=pl.BlockSpec((TM_ROUTE, d), lambda i: (i, 0)),
        out_shape=jax.ShapeDtypeStruct((n, d), F32),
        scratch_shapes=[pltpu.VMEM((2, TOP_K, TM_ROUTE, d // 2), U32), pltpu.SemaphoreType.DMA((2,))],
        compiler_params=pltpu.CompilerParams(
            dimension_semantics=("arbitrary",), vmem_limit_bytes=VMEM_LIMIT),
    )(dest_flat, dest_flat, h, route, ys)


def _pad_lanes(a):
    return jnp.pad(a, ((0, 0), (0, LANES - a.shape[1])))


def kernel(x, attn_norm_w, w_in, b_forget, qn_a, kn_a, qn_b, kn_b, rel_bias, w_branch_a, w_branch_b,
           w_out, ffn_norm_w, w_router, b_router, w_gate_up, b_gate_up, w_down, b_down):
    b, s, d = x.shape
    n = b * s
    x2 = x.reshape(n, d)

    w_all = _w_prep(w_in)
    bf_pad = _pad_lanes(b_forget.reshape(1, N_HEADS))
    scale = HEAD_DIM ** -0.5
    ones = jnp.ones((HEAD_DIM,), F32)
    head_norm_w = jnp.stack([qn_a * scale, kn_a, ones, qn_b * scale, kn_b, ones, ones, ones])
    wr = _pad_lanes(w_router)
    wr_hi = wr.astype(BF16)
    wr_lo = (wr - wr_hi.astype(F32)).astype(BF16)
    br_pad = jnp.concatenate([b_router.reshape(1, N_EXPERTS),
                              jnp.full((1, LANES - N_EXPERTS), NEG_INF, F32)], axis=1)

    proj, cum = _in_proj(x2, attn_norm_w, w_all, w_in, bf_pad, head_norm_w, s)
    proj3 = proj.reshape(b, s, proj.shape[1])

    y_a = _attn_a(proj3, _attn_a_ext_table(rel_bias))
    cum3 = cum.reshape(b, s, LANES)
    cum_rows = jnp.transpose(cum3[:, :, :N_HEADS], (0, 2, 1)).reshape(b, N_HEADS, s // TK_B, 1, TK_B)
    y_b = _attn_b(proj3, cum3, cum_rows)

    h, hn, route, cnt = _merge(x2, y_a.reshape(n, WIDTH), y_b.reshape(n, WIDTH), proj,
                                w_branch_a.astype(BF16), w_branch_b.astype(BF16), w_out.astype(BF16),
                                ffn_norm_w, wr_hi, wr_lo, br_pad)

    cap = n + TM_MOE
    blocks_per_expert = cap // TM_MOE
    top_e = route[:, :TOP_K].astype(jnp.int32)
    rank = route[:, 2 * TOP_K:3 * TOP_K].astype(jnp.int32)
    dest_flat = (top_e * cap + rank).reshape(n * TOP_K)
    counts = cnt[0, :N_EXPERTS].astype(jnp.int32)
    nblk = (counts + TM_MOE - 1) // TM_MOE
    bend = jnp.cumsum(nblk)
    bstart = bend - nblk
    max_blocks = n * TOP_K // TM_MOE + N_EXPERTS
    blk = jnp.minimum(jnp.arange(max_blocks), bend[-1] - 1)
    blk_expert = jnp.minimum(jnp.sum(blk[:, None] >= bend[None, :], axis=1), N_EXPERTS - 1).astype(jnp.int32)
    blk_in_expert = blk - bstart[blk_expert]
    blk_row = (blk_expert * blocks_per_expert + blk_in_expert).astype(jnp.int32)
    blk_tokens = jnp.clip(counts[blk_expert] - blk_in_expert * TM_MOE, 0, TM_MOE)
    blk_chunks = ((blk_tokens + TC_MOE - 1) // TC_MOE).astype(jnp.int32)
    n_blocks = bend[-1].astype(jnp.int32).reshape(1)

    xs = _dispatch(counts, dest_flat, hn, cap)
    ys = _moe(xs, blk_row, blk_expert, blk_chunks, n_blocks, w_gate_up, b_gate_up, w_down, b_down, max_blocks)
    out = _combine(dest_flat, h, route, ys)
    return out.reshape(b, s, d)
```
